```python
import math
import jax, jax.numpy as jnp
from jax import lax
import numpy as np

D_MODEL = 1024
BATCH = 16
SEQ = 4096
DEPTH = 2

F32 = jnp.float32
EPS = 1e-6
N_MEM = 256
N_BRANCH = 4
MAX_POS_OFFSET = 1024

W_A = 512
W_S = 512
S5_GROUP = 16
S5_GROUPS = W_S // S5_GROUP
S5_STATE = 64
MLA_HEADS = 8
Q_LORA = 256
KV_LORA = 256
QK_NOPE = 64
QK_ROPE = 32
V_DIM = 64
ROPE_BASE = 10000.0
Q_BLOCK = 128
W_H = 512
HY_ORDER = 2
HY_EMB = 33
HY_BANDS = (HY_EMB - 1) // 2
HY_FO = 64
HY_FAST_DECAY = 0.3
HY_SLOW_DECAY = 1.5
HY_TARGET = 1e-2
N_IN_A = 3 * W_A
N_IN_S = W_S
N_IN_M = Q_LORA + KV_LORA + QK_ROPE
N_IN_H = 3 * W_H
N_IN_G = N_BRANCH * D_MODEL
N_IN = N_IN_A + N_IN_S + N_IN_M + N_IN_H + N_IN_G
IN_SPLITS = [N_IN_A, N_IN_A + N_IN_S, N_IN_A + N_IN_S + N_IN_M, N_IN_A + N_IN_S + N_IN_M + N_IN_H]
XA_HEADS = 4
XA_DH = 128
N_GROUPS = 4
EXP_PER_GROUP = 8
N_EXPERTS = N_GROUPS * EXP_PER_GROUP
TOP_K = 2
D_FF_E = 256

kernel_name = "hybrid_parallel_gated_encoder"


def rmsnorm(x, g):
    x32 = x.astype(F32)
    y = x32 * lax.rsqrt(jnp.mean(x32 * x32, axis=-1, keepdims=True) + EPS)
    return (y * g.astype(F32)).astype(x.dtype)


def short_conv(u, w):
    up = jnp.pad(u, ((0, 0), (1, 1), (0, 0)))
    return w[0] * up[:, :-2] + w[1] * up[:, 1:-1] + w[2] * up[:, 2:]


def mixer_short_conv(p_a, conv_w, w_out):
    b_gate, c_gate, xin = jnp.split(p_a, 3, axis=-1)
    return (b_gate * short_conv(c_gate * xin, conv_w)) @ w_out


def _linear_recurrence(e1, e2):
    a1, b1 = e1
    a2, b2 = e2
    return a1 * a2, a2 * b1 + b2


def mixer_s5(u, lam_re, lam_im, log_step, b_re, b_im, c_re, c_im, d_skip, w_glu):
    bsz, seqlen, _ = u.shape
    u32 = u.astype(F32)
    ug = u32.reshape(bsz, seqlen, S5_GROUPS, S5_GROUP)
    y = d_skip.astype(F32) * u32
    for direction in range(2):
        lam = lax.complex(lam_re[direction].astype(F32), lam_im[direction].astype(F32))
        delta = jnp.exp(log_step[direction].astype(F32))[:, None]
        lam_bar = jnp.exp(lam * delta)
        b_mat = lax.complex(b_re[direction].astype(F32), b_im[direction].astype(F32))
        b_bar = ((lam_bar - 1.0) / lam)[..., None] * b_mat
        bu = jnp.einsum('blgh,gph->blgp', ug, b_bar)
        a = jnp.broadcast_to(lam_bar, (1, seqlen) + lam_bar.shape)
        _, states = lax.associative_scan(_linear_recurrence, (a, bu), reverse=(direction == 1), axis=1)
        c_mat = lax.complex(c_re[direction].astype(F32), c_im[direction].astype(F32))
        y = y + jnp.real(jnp.einsum('blgp,ghp->blgh', states, c_mat)).reshape(bsz, seqlen, W_S)
    a_half, g_half = jnp.split(jax.nn.gelu(y) @ w_glu.astype(F32), 2, axis=-1)
    return (a_half * jax.nn.sigmoid(g_half)).astype(u.dtype)


def rope(x, cos, sin):
    x1, x2 = jnp.split(x, 2, axis=-1)
    return jnp.concatenate([x1 * cos - x2 * sin, x1 * sin + x2 * cos], axis=-1)


def mixer_mla(p_m, cos, sin, q_norm, w_q_b, kv_norm, w_kv_b, w_o):
    bsz, seqlen, _ = p_m.shape
    c_q, c_kv, k_rope = jnp.split(p_m, [Q_LORA, Q_LORA + KV_LORA], axis=-1)
    q = (rmsnorm(c_q, q_norm) @ w_q_b).reshape(bsz, seqlen, MLA_HEADS, QK_NOPE + QK_ROPE)
    q_nope, q_rope = jnp.split(q, [QK_NOPE], axis=-1)
    q_rope = rope(q_rope, cos[:, :, None], sin[:, :, None])
    k_rope = rope(k_rope, cos, sin)
    kv = (rmsnorm(c_kv, kv_norm) @ w_kv_b).reshape(bsz, seqlen, MLA_HEADS, QK_NOPE + V_DIM)
    k_nope, v = jnp.split(kv, [QK_NOPE], axis=-1)
    scale = (QK_NOPE + QK_ROPE) ** -0.5
    n_blocks = seqlen // Q_BLOCK

    def to_blocks(t):
        return jnp.moveaxis(t.reshape((bsz, n_blocks, Q_BLOCK) + t.shape[2:]), 1, 0)

    def attend(blk):
        qn, qr = blk
        s = jnp.einsum('bqhd,bkhd->bhqk', qn, k_nope) + jnp.einsum('bqhr,bkr->bhqk', qr, k_rope)
        p = jax.nn.softmax(s.astype(F32) * scale, axis=-1).astype(v.dtype)
        return jnp.einsum('bhqk,bkhd->bqhd', p, v)

    o = lax.map(attend, (to_blocks(q_nope), to_blocks(q_rope)))
    o = jnp.moveaxis(o, 0, 1).reshape(bsz, seqlen, MLA_HEADS * V_DIM)
    return o @ w_o


def hyena_filters(seqlen, w1, b1, w2, b2, w3, freq):
    t = jnp.arange(seqlen, dtype=F32)
    t_norm = t / max(seqlen - 1, 1)
    bands = jnp.linspace(1e-4, HY_BANDS - 1, HY_BANDS, dtype=F32)
    ang = (2.0 * math.pi / seqlen) * t[:, None] * bands[None]
    z = jnp.concatenate([t_norm[:, None], jnp.cos(ang), -jnp.sin(ang)], axis=-1)
    fr = freq.astype(F32)
    h = jnp.sin(fr * (z @ w1.astype(F32) + b1.astype(F32)))
    h = jnp.sin(fr * (h @ w2.astype(F32) + b2.astype(F32)))
    filt = (h @ w3.astype(F32)).reshape(seqlen, HY_ORDER, 2, W_H)
    max_decay = math.log(HY_TARGET) / HY_FAST_DECAY
    min_decay = math.log(HY_TARGET) / HY_SLOW_DECAY
    deltas = jnp.abs(jnp.linspace(min_decay, max_decay, W_H, dtype=F32))
    filt = filt * jnp.exp(-t_norm[:, None, None, None] * deltas)
    fwd, bwd = filt[:, :, 0], filt[:, :, 1]
    k = jnp.concatenate([fwd, jnp.zeros_like(fwd[:1]), bwd[:0:-1]], axis=0)
    k = k * lax.rsqrt(jnp.sum(k * k, axis=0, keepdims=True) + EPS)
    return jnp.fft.rfft(k, axis=0)


def fft_conv(u, k_f, bias):
    seqlen = u.shape[1]
    u32 = u.astype(F32)
    u_f = jnp.fft.rfft(u32, n=2 * seqlen, axis=1)
    y = jnp.fft.irfft(u_f * k_f[None], n=2 * seqlen, axis=1)[:, :seqlen]
    return (y + u32 * bias.astype(F32)).astype(u.dtype)


def mixer_hyena(p_h, conv_w, f_w1, f_b1, f_w2, f_b2, f_w3, f_freq, bias, w_out):
    v, x1, x2 = jnp.split(short_conv(p_h, conv_w), 3, axis=-1)
    k_f = hyena_filters(p_h.shape[1], f_w1, f_b1, f_w2, f_b2, f_w3, f_freq)
    z = x1 * fft_conv(v, k_f[:, 0], bias[0])
    z = x2 * fft_conv(z, k_f[:, 1], bias[1])
    return z @ w_out


def cross_attention(h, mem_n, w_q, w_kv, w_o):
    bsz, seqlen, _ = h.shape
    q = (h @ w_q).reshape(bsz, seqlen, XA_HEADS, XA_DH)
    kv = (mem_n @ w_kv).reshape(bsz, mem_n.shape[1], XA_HEADS, 2 * XA_DH)
    k, v = jnp.split(kv, 2, axis=-1)
    s = jnp.einsum('bqhd,bkhd->bhqk', q, k).astype(F32) * (XA_DH ** -0.5)
    p = jax.nn.softmax(s, axis=-1).astype(v.dtype)
    o = jnp.einsum('bhqk,bkhd->bqhd', p, v).reshape(bsz, seqlen, XA_HEADS * XA_DH)
    return o @ w_o


def hier_moe(h, w_group, b_group, w_expert, b_expert, w_gate, w_up, w_down):
    bsz, seqlen, d = h.shape
    t = h.reshape(-1, d)
    g_logits = (t @ w_group).astype(F32) + b_group.astype(F32)
    g_prob = jax.nn.softmax(g_logits, axis=-1)
    g_idx = jnp.argmax(g_logits, axis=-1)
    g_w = jnp.take_along_axis(g_prob, g_idx[:, None], axis=-1)
    e_logits = ((t @ w_expert).astype(F32) + b_expert.astype(F32)).reshape(-1, N_GROUPS, EXP_PER_GROUP)
    e_logits = jnp.take_along_axis(e_logits, g_idx[:, None, None], axis=1)[:, 0]
    top_v, top_i = lax.top_k(e_logits, TOP_K)
    e_w = jax.nn.softmax(top_v, axis=-1) * g_w
    expert_id = g_idx[:, None] * EXP_PER_GROUP + top_i
    combine = jnp.sum(jax.nn.one_hot(expert_id, N_EXPERTS, dtype=F32) * e_w[..., None], axis=1).astype(h.dtype)
    out = jnp.zeros_like(t)
    for g in range(N_GROUPS):
        sl = slice(g * EXP_PER_GROUP, (g + 1) * EXP_PER_GROUP)
        hid = jax.nn.silu(jnp.einsum('td,edf->tef', t, w_gate[sl])) * jnp.einsum('td,edf->tef', t, w_up[sl])
        out = out + jnp.einsum('tef,efd->td', hid * combine[:, sl, None], w_down[sl])
    return out.reshape(bsz, seqlen, d)


def setup_inputs(seed: int = 0) -> dict:
    key = jax.random.key(seed)
    keys = jax.random.split(key, 80)
    counter = [0]

    def nxt():
        k = keys[counter[0]]
        counter[0] += 1
        return k

    def nrm(shape, scale):
        return jax.random.normal(nxt(), shape, F32) * scale

    def gain(shape):
        return 1.0 + nrm(shape, 0.02)

    L = DEPTH
    G, P = S5_GROUPS, S5_STATE
    inp = {}
    inp['x'] = nrm((BATCH, SEQ, D_MODEL), 1.0)
    inp['mem'] = nrm((BATCH, N_MEM, D_MODEL), 1.0)
    inp['positions'] = (jax.random.randint(nxt(), (BATCH, 1), 0, MAX_POS_OFFSET, dtype=jnp.int32)
                        + jnp.arange(SEQ, dtype=jnp.int32)[None])
    inp['mix_norm'] = gain((L, D_MODEL))
    inp['w_in'] = nrm((L, D_MODEL, N_IN), D_MODEL ** -0.5)
    inp['gate_bias'] = nrm((L, N_BRANCH, D_MODEL), 0.1)
    inp['conv_a'] = nrm((L, 3, W_A), 3 ** -0.5)
    inp['w_out_a'] = nrm((L, W_A, D_MODEL), W_A ** -0.5)
    inp['s5_lambda_re'] = -0.5 + nrm((L, 2, G, P), 0.01)
    inp['s5_lambda_im'] = math.pi * jnp.arange(P, dtype=F32) + nrm((L, 2, G, P), 0.01)
    inp['s5_log_step'] = jax.random.uniform(nxt(), (L, 2, G), F32, math.log(1e-3), math.log(1e-1))
    inp['s5_b_re'] = nrm((L, 2, G, P, S5_GROUP), (2 * S5_GROUP) ** -0.5)
    inp['s5_b_im'] = nrm((L, 2, G, P, S5_GROUP), (2 * S5_GROUP) ** -0.5)
    inp['s5_c_re'] = nrm((L, 2, G, S5_GROUP, P), (2 * P) ** -0.5)
    inp['s5_c_im'] = nrm((L, 2, G, S5_GROUP, P), (2 * P) ** -0.5)
    inp['s5_d'] = nrm((L, W_S), 1.0)
    inp['s5_w_glu'] = nrm((L, W_S, 2 * D_MODEL), W_S ** -0.5)
    inp['mla_q_norm'] = gain((L, Q_LORA))
    inp['mla_w_q_b'] = nrm((L, Q_LORA, MLA_HEADS * (QK_NOPE + QK_ROPE)), Q_LORA ** -0.5)
    inp['mla_kv_norm'] = gain((L, KV_LORA))
    inp['mla_w_kv_b'] = nrm((L, KV_LORA, MLA_HEADS * (QK_NOPE + V_DIM)), KV_LORA ** -0.5)
    inp['mla_w_o'] = nrm((L, MLA_HEADS * V_DIM, D_MODEL), (MLA_HEADS * V_DIM) ** -0.5)
    inp['hy_conv'] = nrm((L, 3, 3 * W_H), 3 ** -0.5)
    inp['hy_f_w1'] = nrm((L, HY_EMB, HY_FO), HY_EMB ** -0.5)
    inp['hy_f_b1'] = nrm((L, HY_FO), 0.1)
    inp['hy_f_w2'] = nrm((L, HY_FO, HY_FO), HY_FO ** -0.5)
    inp['hy_f_b2'] = nrm((L, HY_FO), 0.1)
    inp['hy_f_w3'] = nrm((L, HY_FO, HY_ORDER * 2 * W_H), HY_FO ** -0.5)
    inp['hy_f_freq'] = gain((L, HY_FO))
    inp['hy_bias'] = nrm((L, HY_ORDER, W_H), 0.5)
    inp['hy_w_out'] = nrm((L, W_H, D_MODEL), W_H ** -0.5)
    inp['w_mix_out'] = nrm((L, D_MODEL, D_MODEL), D_MODEL ** -0.5)
    inp['xa_norm'] = gain((L, D_MODEL))
    inp['mem_norm'] = gain((D_MODEL,))
    inp['xa_w_q'] = nrm((L, D_MODEL, XA_HEADS * XA_DH), D_MODEL ** -0.5)
    inp['xa_w_kv'] = nrm((L, D_MODEL, 2 * XA_HEADS * XA_DH), D_MODEL ** -0.5)
    inp['xa_w_o'] = nrm((L, XA_HEADS * XA_DH, D_MODEL), (XA_HEADS * XA_DH) ** -0.5)
    inp['moe_norm'] = gain((L, D_MODEL))
    inp['moe_w_group'] = nrm((L, D_MODEL, N_GROUPS), D_MODEL ** -0.5)
    inp['moe_b_group'] = nrm((L, N_GROUPS), 0.01)
    inp['moe_w_expert'] = nrm((L, D_MODEL, N_EXPERTS), D_MODEL ** -0.5)
    inp['moe_b_expert'] = nrm((L, N_EXPERTS), 0.01)
    inp['moe_w_gate'] = nrm((L, N_EXPERTS, D_MODEL, D_FF_E), D_MODEL ** -0.5)
    inp['moe_w_up'] = nrm((L, N_EXPERTS, D_MODEL, D_FF_E), D_MODEL ** -0.5)
    inp['moe_w_down'] = nrm((L, N_EXPERTS, D_FF_E, D_MODEL), D_FF_E ** -0.5)
    inp['final_norm'] = gain((D_MODEL,))
    return inp


def reference(x, mem, positions, mix_norm, w_in, gate_bias, conv_a, w_out_a,
              s5_lambda_re, s5_lambda_im, s5_log_step, s5_b_re, s5_b_im, s5_c_re, s5_c_im, s5_d, s5_w_glu,
              mla_q_norm, mla_w_q_b, mla_kv_norm, mla_w_kv_b, mla_w_o,
              hy_conv, hy_f_w1, hy_f_b1, hy_f_w2, hy_f_b2, hy_f_w3, hy_f_freq, hy_bias, hy_w_out,
              w_mix_out, xa_norm, mem_norm, xa_w_q, xa_w_kv, xa_w_o,
              moe_norm, moe_w_group, moe_b_group, moe_w_expert, moe_b_expert, moe_w_gate, moe_w_up, moe_w_down,
              final_norm):
    inv_freq = 1.0 / (ROPE_BASE ** (jnp.arange(0, QK_ROPE, 2, dtype=F32) / QK_ROPE))
    ang = positions.astype(F32)[..., None] * inv_freq
    cos = jnp.cos(ang).astype(x.dtype)
    sin = jnp.sin(ang).astype(x.dtype)
    mem_n = rmsnorm(mem, mem_norm)
    for l in range(DEPTH):
        h = rmsnorm(x, mix_norm[l])
        p_a, p_s, p_m, p_h, p_g = jnp.split(h @ w_in[l], IN_SPLITS, axis=-1)
        y_a = mixer_short_conv(p_a, conv_a[l], w_out_a[l])
        y_s = mixer_s5(p_s, s5_lambda_re[l], s5_lambda_im[l], s5_log_step[l], s5_b_re[l], s5_b_im[l],
                       s5_c_re[l], s5_c_im[l], s5_d[l], s5_w_glu[l])
        y_m = mixer_mla(p_m, cos, sin, mla_q_norm[l], mla_w_q_b[l], mla_kv_norm[l], mla_w_kv_b[l], mla_w_o[l])
        y_h = mixer_hyena(p_h, hy_conv[l], hy_f_w1[l], hy_f_b1[l], hy_f_w2[l], hy_f_b2[l], hy_f_w3[l],
                          hy_f_freq[l], hy_bias[l], hy_w_out[l])
        gates = jax.nn.sigmoid(p_g.reshape(p_g.shape[:-1] + (N_BRANCH, D_MODEL)) + gate_bias[l])
        merged = (gates[..., 0, :] * y_a + gates[..., 1, :] * y_s
                  + gates[..., 2, :] * y_m + gates[..., 3, :] * y_h)
        x = x + merged @ w_mix_out[l]
        x = x + cross_attention(rmsnorm(x, xa_norm[l]), mem_n, xa_w_q[l], xa_w_kv[l], xa_w_o[l])
        x = x + hier_moe(rmsnorm(x, moe_norm[l]), moe_w_group[l], moe_b_group[l], moe_w_expert[l],
                         moe_b_expert[l], moe_w_gate[l], moe_w_up[l], moe_w_down[l])
    return rmsnorm(x, final_norm)
```

```python
import functools
import math

import numpy as np
import jax
import jax.numpy as jnp
from jax import lax
from jax.experimental import pallas as pl
from jax.experimental.pallas import tpu as pltpu

F32 = jnp.float32
BF16 = jnp.bfloat16
EPS = 1e-6

D_MODEL = 1024
N_BRANCH = 4
W_A = 512
W_S = 512
S5_GROUP = 16
S5_GROUPS = W_S // S5_GROUP
S5_STATE = 64
S5_CHUNK = 16
MLA_HEADS = 8
Q_LORA = 256
KV_LORA = 256
QK_NOPE = 64
QK_ROPE = 32
V_DIM = 64
ROPE_BASE = 10000.0
W_H = 512
HY_ORDER = 2
HY_EMB = 33
HY_BANDS = (HY_EMB - 1) // 2
HY_FO = 64
HY_FAST_DECAY = 0.3
HY_SLOW_DECAY = 1.5
HY_TARGET = 1e-2
XA_HEADS = 4
XA_DH = 128
N_GROUPS = 4
EXP_PER_GROUP = 8
N_EXPERTS = N_GROUPS * EXP_PER_GROUP
D_FF_E = 256

LANE = 128
HEAD_PAD = 128
VMEM_LIMIT = 56 * 1024 * 1024

OFF_G = 0
OFF_A = OFF_G + N_BRANCH * D_MODEL
OFF_H = OFF_A + 3 * W_A
OFF_S = OFF_H + 3 * W_H
OFF_CQ = OFF_S + W_S
OFF_CKV = OFF_CQ + Q_LORA
OFF_KRA = OFF_CKV + KV_LORA
OFF_KRB = OFF_KRA + LANE
N_P = OFF_KRB + LANE

_IN_A = 0
_IN_S = 3 * W_A
_IN_M = _IN_S + W_S
_IN_H = _IN_M + Q_LORA + KV_LORA + QK_ROPE
_IN_G = _IN_H + 3 * W_H

HY_N1 = 8
HY_CT = 128
HY_NCT = W_H // HY_CT


def _cparams(sem, vmem=VMEM_LIMIT):
    return pltpu.CompilerParams(dimension_semantics=sem, vmem_limit_bytes=vmem)


def _split_bf16(x):
    hi = x.astype(BF16)
    lo = (x - hi.astype(F32)).astype(BF16)
    return hi, lo


def _dot(a, b):
    return jnp.dot(a, b, preferred_element_type=F32)


def _dot3(a_hi, a_lo, b_hi, b_lo):
    return _dot(a_hi, b_hi) + (_dot(a_lo, b_hi) + _dot(a_hi, b_lo))


def _rms(x, g):
    return x * lax.rsqrt(jnp.mean(x * x, axis=-1, keepdims=True) + EPS) * g


def _inproj_kernel(x_ref, g_ref, w_ref, o_ref, h_ref):
    @pl.when(pl.program_id(1) == 0)
    def _():
        h_ref[...] = _rms(x_ref[...], g_ref[...]).astype(BF16)

    o_ref[...] = _dot(h_ref[...], w_ref[...]).astype(o_ref.dtype)


def _inproj(x2d, g, w, *, tm=1024, tn=768):
    t = x2d.shape[0]
    return pl.pallas_call(
        _inproj_kernel,
        grid=(t // tm, N_P // tn),
        in_specs=[pl.BlockSpec((tm, D_MODEL), lambda i, j: (i, 0)),
                  pl.BlockSpec((1, D_MODEL), lambda i, j: (0, 0)),
                  pl.BlockSpec((D_MODEL, tn), lambda i, j: (0, j))],
        out_specs=pl.BlockSpec((tm, tn), lambda i, j: (i, j)),
        out_shape=jax.ShapeDtypeStruct((t, N_P), F32),
        scratch_shapes=[pltpu.VMEM((tm, D_MODEL), BF16)],
        compiler_params=_cparams(("parallel", "arbitrary")),
        name="inproj",
    )(x2d, g, w)


def _conv3(u, w):
    n = u.shape[0]
    row = lax.broadcasted_iota(jnp.int32, u.shape, 0)
    prev = jnp.where(row == 0, 0.0, pltpu.roll(u, 1, axis=0))
    nxt = jnp.where(row == n - 1, 0.0, pltpu.roll(u, n - 1, axis=0))
    return w[0:1] * prev + w[1:2] * u + w[2:3] * nxt


def _shortconv_kernel(bg_ref, cg_ref, xi_ref, hv_ref, h1_ref, h2_ref, wa_ref, wh_ref,
                      a_ref, v_ref, x1_ref, x2_ref):
    a_ref[0] = bg_ref[0] * _conv3(cg_ref[0] * xi_ref[0], wa_ref[0])
    v_ref[0, 0] = _conv3(hv_ref[0], wh_ref[0, 0])
    x1_ref[0, 0] = _conv3(h1_ref[0], wh_ref[0, 1])
    x2_ref[0, 0] = _conv3(h2_ref[0], wh_ref[0, 2])


def _shortconv(p3, conv_a, hy_conv):
    b, l, _ = p3.shape
    nct = W_A // LANE
    wa = conv_a.reshape(3, nct, LANE).transpose(1, 0, 2)
    wh = hy_conv.reshape(3, 3, HY_NCT, HY_CT).transpose(2, 1, 0, 3)

    def pspec(off):
        return pl.BlockSpec((1, l, LANE), lambda i, j, off=off: (i, 0, off // LANE + j))

    tiled = jax.ShapeDtypeStruct((b, HY_NCT, l, HY_CT), F32)
    tspec = pl.BlockSpec((1, 1, l, HY_CT), lambda i, j: (i, j, 0, 0))
    return pl.pallas_call(
        _shortconv_kernel,
        grid=(b, nct),
        in_specs=[pspec(OFF_A), pspec(OFF_A + W_A), pspec(OFF_A + 2 * W_A),
                  pspec(OFF_H), pspec(OFF_H + W_H), pspec(OFF_H + 2 * W_H),
                  pl.BlockSpec((1, 3, LANE), lambda i, j: (j, 0, 0)),
                  pl.BlockSpec((1, 3, 3, HY_CT), lambda i, j: (j, 0, 0, 0))],
        out_specs=[pl.BlockSpec((1, l, LANE), lambda i, j: (i, 0, j)), tspec, tspec, tspec],
        out_shape=[jax.ShapeDtypeStruct((b, l, W_A), F32), tiled, tiled, tiled],
        compiler_params=_cparams(("parallel", "parallel")),
        name="shortconv",
    )(p3, p3, p3, p3, p3, p3, wa, wh)


def _s5_operators(lam_re, lam_im, log_step, b_re, b_im, c_re, c_im, d_skip):
    q, hh, p = S5_CHUNK, S5_GROUP, S5_STATE
    lam = lax.complex(lam_re, lam_im)
    delta = jnp.exp(log_step)[..., None]
    lam_bar = jnp.exp(lam * delta)
    b_bar = ((lam_bar - 1.0) / lam)[..., None] * lax.complex(b_re, b_im)
    c_mat = lax.complex(c_re, c_im)
    steps = jnp.arange(q + 1, dtype=F32)
    powers = jnp.exp((lam * delta)[..., None] * steps)

    i_idx = jnp.arange(q)[:, None]
    j_idx = jnp.arange(q)[None, :]
    kern = jnp.real(jnp.einsum('dghp,dgpe,dgpk->dgehk', c_mat, powers[..., :q], b_bar))
    diff_f = jnp.clip(j_idx - i_idx, 0, q - 1)
    diff_b = jnp.clip(i_idx - j_idx, 0, q - 1)
    t_f = jnp.where((i_idx <= j_idx)[None, :, :, None, None], kern[0][:, diff_f], 0.0)
    t_b = jnp.where((i_idx >= j_idx)[None, :, :, None, None], kern[1][:, diff_b], 0.0)
    t_all = (t_f + t_b).transpose(0, 1, 4, 2, 3)
    g = lam_re.shape[1]
    t_all = t_all.reshape(g, q * hh, q * hh)
    d_vec = jnp.tile(d_skip.reshape(g, 1, hh), (1, q, 1)).reshape(g, 1, q * hh)
    t_all = t_all + jnp.eye(q * hh, dtype=F32)[None] * d_vec

    pw_f = powers[0][..., :q][..., ::-1]
    pw_b = powers[1][..., :q]
    st_f = jnp.einsum('gpi,gph->gihp', pw_f, b_bar[0]).reshape(g, q * hh, p)
    st_b = jnp.einsum('gpi,gph->gihp', pw_b, b_bar[1]).reshape(g, q * hh, p)
    wst = jnp.concatenate([jnp.real(st_f), jnp.real(st_b), jnp.imag(st_f), jnp.imag(st_b)], axis=-1)
    w1 = jnp.concatenate([t_all, wst], axis=-1)

    pw_of = powers[0][..., 1:]
    pw_ob = powers[1][..., 1:][..., ::-1]
    o_f = jnp.einsum('ghp,gpj->gpjh', c_mat[0], pw_of).reshape(g, p, q * hh)
    o_b = jnp.einsum('ghp,gpj->gpjh', c_mat[1], pw_ob).reshape(g, p, q * hh)
    wout = jnp.concatenate([jnp.real(o_f), jnp.real(o_b), -jnp.imag(o_f), -jnp.imag(o_b)], axis=1)

    lam_q = powers[..., q]
    lam_c = jnp.stack([jnp.concatenate([jnp.real(lam_q[0]), jnp.real(lam_q[1])], -1),
                       jnp.concatenate([jnp.imag(lam_q[0]), jnp.imag(lam_q[1])], -1)], axis=1)
    return w1.astype(F32), wout.astype(F32), lam_c.astype(F32)


def _s5_kernel(u_ref, w1h_ref, w1l_ref, woh_ref, wol_ref, lam_ref, y_ref,
               yin_ref, sre_ref, sim_ref, xre_ref, xim_ref, *, n_chunks, bsz):
    half = LANE // 2
    u_hi, u_lo = _split_bf16(u_ref[0])
    m1 = _dot3(u_hi, u_lo, w1h_ref[0], w1l_ref[0])
    yin_ref[...] = m1[:, :2 * LANE]
    sre_ref[...] = m1[:, 2 * LANE:3 * LANE]
    sim_ref[...] = m1[:, 3 * LANE:]
    lam = lam_ref[0]
    lr = jnp.broadcast_to(lam[0:1], (bsz, LANE))
    li = jnp.broadcast_to(lam[1:2], (bsz, LANE))
    fwd_lane = lax.broadcasted_iota(jnp.int32, (bsz, LANE), 1) < half

    def step(k, carry):
        xr, xi = carry
        rf = pl.multiple_of(k * bsz, bsz)
        rb = pl.multiple_of((n_chunks - 1 - k) * bsz, bsz)
        xre_ref[pl.ds(rf, bsz), 0:half] = xr[:, 0:half]
        xim_ref[pl.ds(rf, bsz), 0:half] = xi[:, 0:half]
        xre_ref[pl.ds(rb, bsz), half:LANE] = xr[:, half:LANE]
        xim_ref[pl.ds(rb, bsz), half:LANE] = xi[:, half:LANE]
        ar = jnp.where(fwd_lane, sre_ref[pl.ds(rf, bsz), :], sre_ref[pl.ds(rb, bsz), :])
        ai = jnp.where(fwd_lane, sim_ref[pl.ds(rf, bsz), :], sim_ref[pl.ds(rb, bsz), :])
        return lr * xr - li * xi + ar, lr * xi + li * xr + ai

    zero = jnp.zeros((bsz, LANE), F32)
    lax.fori_loop(0, n_chunks, step, (zero, zero))
    xs = jnp.concatenate([xre_ref[...], xim_ref[...]], axis=-1)
    x_hi, x_lo = _split_bf16(xs)
    y_ref[0] = yin_ref[...] + _dot3(x_hi, x_lo, woh_ref[0], wol_ref[0])


def _s5(p3, ops):
    b, l, _ = p3.shape
    q, hh, g = S5_CHUNK, S5_GROUP, S5_GROUPS
    nc = l // q
    rows = nc * b
    w1, wout, lam = ops
    u = p3[:, :, OFF_S:OFF_S + W_S].reshape(b, nc, q, g, hh)
    u = u.transpose(3, 1, 0, 2, 4).reshape(g, rows, q * hh)
    w1h, w1l = _split_bf16(w1)
    woh, wol = _split_bf16(wout)
    wspec = lambda n: pl.BlockSpec((1, q * hh, n), lambda i: (i, 0, 0))
    y = pl.pallas_call(
        functools.partial(_s5_kernel, n_chunks=nc, bsz=b),
        grid=(g,),
        in_specs=[pl.BlockSpec((1, rows, q * hh), lambda i: (i, 0, 0)),
                  wspec(4 * LANE), wspec(4 * LANE), wspec(2 * LANE), wspec(2 * LANE),
                  pl.BlockSpec((1, 2, LANE), lambda i: (i, 0, 0))],
        out_specs=pl.BlockSpec((1, rows, q * hh), lambda i: (i, 0, 0)),
        out_shape=jax.ShapeDtypeStruct((g, rows, q * hh), F32),
        scratch_shapes=[pltpu.VMEM((rows, 2 * LANE), F32), pltpu.VMEM((rows, LANE), F32),
                        pltpu.VMEM((rows, LANE), F32), pltpu.VMEM((rows, LANE), F32),
                        pltpu.VMEM((rows, LANE), F32)],
        compiler_params=_cparams(("parallel",)),
        name="s5_scan",
    )(u, w1h, w1l, woh, wol, lam)
    y = y.reshape(g, nc, b, q, hh).transpose(2, 1, 3, 0, 4)
    return y.reshape(b, l, W_S)


def _mla_prep_kernel(cq_ref, ckv_ref, kra_ref, krb_ref, cos_ref, sin_ref, qn_ref, kvn_ref,
                     wq_ref, wkv_ref, q_ref, k_ref, v_ref):
    cqn = _rms(cq_ref[0], qn_ref[...]).astype(BF16)
    ckvn = _rms(ckv_ref[0], kvn_ref[...]).astype(BF16)
    cos = cos_ref[0]
    sin = sin_ref[0]
    kr = kra_ref[0] * cos + krb_ref[0] * sin
    lane = lax.broadcasted_iota(jnp.int32, cos.shape, 1)
    ones_col = jnp.where(lane == V_DIM, 1.0, 0.0)
    scale = (QK_NOPE + QK_ROPE) ** -0.5
    for h in range(MLA_HEADS):
        qq = _dot(cqn, wq_ref[h])
        q_ref[0, h] = ((qq[:, :HEAD_PAD] * cos + qq[:, HEAD_PAD:] * sin) * scale).astype(BF16)
        kv = _dot(ckvn, wkv_ref[h])
        k_ref[0, h] = (kv[:, :HEAD_PAD] + kr).astype(BF16)
        v_ref[0, h] = (kv[:, HEAD_PAD:] + ones_col).astype(BF16)


def _mla_weights(w_q_b, w_kv_b):
    dq = QK_NOPE + QK_ROPE
    half = QK_ROPE // 2
    wq = w_q_b.reshape(Q_LORA, MLA_HEADS, dq).transpose(1, 0, 2)
    rope = wq[..., QK_NOPE:]
    rot = jnp.concatenate([-rope[..., half:], rope[..., :half]], axis=-1)
    zq = jnp.zeros((MLA_HEADS, Q_LORA, HEAD_PAD - dq), F32)
    zn = jnp.zeros((MLA_HEADS, Q_LORA, QK_NOPE), F32)
    wq_full = jnp.concatenate([wq, zq, zn, rot, zq], axis=-1)
    wkv = w_kv_b.reshape(KV_LORA, MLA_HEADS, QK_NOPE + V_DIM).transpose(1, 0, 2)
    zk = jnp.zeros((MLA_HEADS, KV_LORA, HEAD_PAD - QK_NOPE), F32)
    zv = jnp.zeros((MLA_HEADS, KV_LORA, HEAD_PAD - V_DIM), F32)
    wkv_full = jnp.concatenate([wkv[..., :QK_NOPE], zk, wkv[..., QK_NOPE:], zv], axis=-1)
    return wq_full.astype(BF16), wkv_full.astype(BF16)


def _mla_prep(p3, cos_t, sin_t, q_norm, kv_norm, wq, wkv, *, tl=512):
    b, l, _ = p3.shape
    hspec = pl.BlockSpec((1, MLA_HEADS, tl, HEAD_PAD), lambda i, j: (i, 0, j, 0))
    hshape = jax.ShapeDtypeStruct((b, MLA_HEADS, l, HEAD_PAD), BF16)
    tab = pl.BlockSpec((1, tl, LANE), lambda i, j: (i, j, 0))
    return pl.pallas_call(
        _mla_prep_kernel,
        grid=(b, l // tl),
        in_specs=[pl.BlockSpec((1, tl, Q_LORA), lambda i, j: (i, j, OFF_CQ // Q_LORA)),
                  pl.BlockSpec((1, tl, KV_LORA), lambda i, j: (i, j, OFF_CKV // KV_LORA)),
                  pl.BlockSpec((1, tl, LANE), lambda i, j: (i, j, OFF_KRA // LANE)),
                  pl.BlockSpec((1, tl, LANE), lambda i, j: (i, j, OFF_KRB // LANE)),
                  tab, tab,
                  pl.BlockSpec((1, Q_LORA), lambda i, j: (0, 0)),
                  pl.BlockSpec((1, KV_LORA), lambda i, j: (0, 0)),
                  pl.BlockSpec((MLA_HEADS, Q_LORA, 2 * HEAD_PAD), lambda i, j: (0, 0, 0)),
                  pl.BlockSpec((MLA_HEADS, KV_LORA, 2 * HEAD_PAD), lambda i, j: (0, 0, 0))],
        out_specs=[hspec, hspec, hspec],
        out_shape=[hshape, hshape, hshape],
        compiler_params=_cparams(("parallel", "parallel")),
        name="mla_prep",
    )(p3, p3, p3, p3, cos_t, sin_t, q_norm, kv_norm, wq, wkv)


def _mla_attn_kernel(q_ref, k_ref, v_ref, o_ref):
    outs = []
    for h in range(2):
        s = lax.dot_general(q_ref[0, h], k_ref[0, h], (((1,), (1,)), ((), ())),
                            preferred_element_type=F32)
        m = jnp.max(s, axis=-1, keepdims=True)
        p = jnp.exp(s - m).astype(BF16)
        o = _dot(p, v_ref[0, h])
        outs.append(o / o[:, V_DIM:V_DIM + 1])
    lane = lax.broadcasted_iota(jnp.int32, outs[0].shape, 1)
    o_ref[0] = jnp.where(lane < V_DIM, outs[0], pltpu.roll(outs[1], V_DIM, axis=1))


def _mla_attn(q, k, v, *, tq=256):
    b, _, l, _ = q.shape
    kvspec = pl.BlockSpec((1, 2, l, HEAD_PAD), lambda i, j, t: (i, j, 0, 0))
    return pl.pallas_call(
        _mla_attn_kernel,
        grid=(b, MLA_HEADS // 2, l // tq),
        in_specs=[pl.BlockSpec((1, 2, tq, HEAD_PAD), lambda i, j, t: (i, j, t, 0)), kvspec, kvspec],
        out_specs=pl.BlockSpec((1, tq, 2 * V_DIM), lambda i, j, t: (i, t, j)),
        out_shape=jax.ShapeDtypeStruct((b, l, MLA_HEADS * V_DIM), F32),
        compiler_params=_cparams(("parallel", "parallel", "arbitrary")),
        name="mla_attn",
    )(q, k, v)


def _hy_sizes(l):
    n = 2 * l
    n2 = n // HY_N1
    nf = n2 // 2 + 1
    nfp = ((nf + 63) // 64) * 64
    return n, n2, nf, nfp


def _hy_tables(l):
    n, n2, nf, nfp = _hy_sizes(l)
    f2 = np.arange(nfp)[:, None].astype(np.float64)
    t2 = np.arange(n2)[None, :].astype(np.float64)
    valid = (np.arange(nfp) < nf)[:, None]
    ang = 2.0 * np.pi * f2 * t2 / n2
    fwd = np.concatenate([np.where(valid, np.cos(ang), 0.0), np.where(valid, -np.sin(ang), 0.0)], axis=0)
    wgt = np.where((np.arange(nfp) == 0) | (np.arange(nfp) == nf - 1), 1.0, 2.0)[:, None] * valid / n
    th = ang[:, :n2 // 2]
    inv = np.concatenate([(wgt * np.cos(th)).T, (-wgt * np.sin(th)).T], axis=1)
    t1 = np.arange(HY_N1)[None, :].astype(np.float64)
    tw_ang = 2.0 * np.pi * f2 * t1 / n
    tw_re = np.repeat(np.cos(tw_ang), HY_CT, axis=1)
    tw_im = np.repeat(-np.sin(tw_ang), HY_CT, axis=1)
    return (jnp.asarray(fwd, F32), jnp.asarray(inv, F32),
            jnp.asarray(tw_re, F32), jnp.asarray(tw_im, F32))


def _cmul(a, b):
    return a[0] * b[0] - a[1] * b[1], a[0] * b[1] + a[1] * b[0]


def _cadd(a, b):
    return a[0] + b[0], a[1] + b[1]


def _csub(a, b):
    return a[0] - b[0], a[1] - b[1]


def _cmul_i(a, sign):
    return (-a[1], a[0]) if sign > 0 else (a[1], -a[0])


def _fft4(a, sign):
    s0, s1 = _cadd(a[0], a[2]), _csub(a[0], a[2])
    s2, s3 = _cadd(a[1], a[3]), _csub(a[1], a[3])
    r3 = _cmul_i(s3, sign)
    return [_cadd(s0, s2), _cadd(s1, r3), _csub(s0, s2), _csub(s1, r3)]


def _fft8(x, sign):
    e = _fft4([x[0], x[2], x[4], x[6]], sign)
    o = _fft4([x[1], x[3], x[5], x[7]], sign)
    r = math.sqrt(0.5)
    o1 = ((o[1][0] - sign * o[1][1]) * r, (o[1][1] + sign * o[1][0]) * r)
    o2 = _cmul_i(o[2], sign)
    o3 = ((-o[3][0] - sign * o[3][1]) * r, (-o[3][1] + sign * o[3][0]) * r)
    tw = [o[0], o1, o2, o3]
    return [_cadd(e[k], tw[k]) for k in range(4)] + [_csub(e[k], tw[k]) for k in range(4)]


def _blocks(ref_re, ref_im, rows):
    return [(ref_re[rows, k * HY_CT:(k + 1) * HY_CT], ref_im[rows, k * HY_CT:(k + 1) * HY_CT])
            for k in range(HY_N1)]


def _hy_spectrum(z_ref, twr_ref, twi_ref, rows, nfp):
    t = []
    for k in range(HY_N1):
        sl = slice(k * HY_CT, (k + 1) * HY_CT)
        zk = (z_ref[rows, sl], z_ref[pl.ds(nfp + rows.start, rows.size), sl])
        t.append(_cmul(zk, (twr_ref[rows, sl], twi_ref[rows, sl])))
    return _fft8(t, -1)


def _hy_filter_mlp_kernel(z_ref, w1_ref, b1_ref, w2_ref, b2_ref, w3_ref, fr_ref, tn_ref, dl_ref, o_ref):
    hp = lax.Precision.HIGHEST
    fr = fr_ref[...]
    h = jnp.sin(fr * (jnp.dot(z_ref[...], w1_ref[...], precision=hp, preferred_element_type=F32) + b1_ref[...]))
    h = jnp.sin(fr * (jnp.dot(h, w2_ref[...], precision=hp, preferred_element_type=F32) + b2_ref[...]))
    filt = jnp.dot(h, w3_ref[...], precision=hp, preferred_element_type=F32)
    decay = jnp.exp(-tn_ref[...] * dl_ref[...])
    o_ref[...] = filt * jnp.concatenate([decay] * (2 * HY_ORDER), axis=-1)


def _hy_filter_mlp(l, w1, b1, w2, b2, w3, freq, *, tl=512):
    t = np.arange(l, dtype=np.float32)
    t_norm = t / np.float32(max(l - 1, 1))
    bands = np.linspace(1e-4, HY_BANDS - 1, HY_BANDS, dtype=np.float32)
    ang = np.float32(2.0 * math.pi / l) * t[:, None] * bands[None]
    z = np.concatenate([t_norm[:, None], np.cos(ang), -np.sin(ang)], axis=-1).astype(np.float32)
    kpad = 40
    z = np.pad(z, ((0, 0), (0, kpad - HY_EMB)))
    w1p = jnp.pad(w1, ((0, kpad - HY_EMB), (0, 0)))
    max_decay = math.log(HY_TARGET) / HY_FAST_DECAY
    min_decay = math.log(HY_TARGET) / HY_SLOW_DECAY
    deltas = np.abs(np.linspace(min_decay, max_decay, W_H, dtype=np.float32))[None]
    full = lambda s: pl.BlockSpec(s, lambda i: (0, 0))
    return pl.pallas_call(
        _hy_filter_mlp_kernel,
        grid=(l // tl,),
        in_specs=[pl.BlockSpec((tl, kpad), lambda i: (i, 0)), full((kpad, HY_FO)), full((1, HY_FO)),
                  full((HY_FO, HY_FO)), full((1, HY_FO)), full((HY_FO, 2 * HY_ORDER * W_H)),
                  full((1, HY_FO)), pl.BlockSpec((tl, 1), lambda i: (i, 0)), full((1, W_H))],
        out_specs=pl.BlockSpec((tl, 2 * HY_ORDER * W_H), lambda i: (i, 0)),
        out_shape=jax.ShapeDtypeStruct((l, 2 * HY_ORDER * W_H), F32),
        compiler_params=_cparams(("parallel",)),
        name="hyena_filter_mlp",
    )(jnp.asarray(z), w1p, b1[None], w2, b2[None], w3, freq[None], jnp.asarray(t_norm[:, None]),
      jnp.asarray(deltas))


def _hy_kf_kernel(k_ref, fh_ref, fl_ref, twr_ref, twi_ref, kr_ref, ki_ref, z_ref, *, nfp, row_chunk):
    kk = k_ref[0, 0]
    sq = jnp.sum(kk * kk, axis=0, keepdims=True)
    ss = sq[:, 0:HY_CT]
    for k in range(1, HY_N1):
        ss = ss + sq[:, k * HY_CT:(k + 1) * HY_CT]
    inv = lax.rsqrt(ss + EPS)
    k_hi, k_lo = _split_bf16(kk)
    z_ref[...] = _dot3(fh_ref[...], fl_ref[...], k_hi, k_lo)
    for c in range(nfp // row_chunk):
        rows = pl.ds(c * row_chunk, row_chunk)
        spec = _hy_spectrum(z_ref, twr_ref, twi_ref, rows, nfp)
        for f1 in range(HY_N1):
            sl = slice(f1 * HY_CT, (f1 + 1) * HY_CT)
            kr_ref[0, 0, rows, sl] = spec[f1][0] * inv
            ki_ref[0, 0, rows, sl] = spec[f1][1] * inv


def _hy_row_chunk(nfp):
    for c in (96, 72, 64, 48, 32, 16, 8):
        if nfp % c == 0:
            return c
    return nfp


def _hy_filter_spectrum(filt, l, tables):
    n, n2, nf, nfp = _hy_sizes(l)
    fwd_t, _, tw_re, tw_im = tables
    f4 = filt.reshape(l, HY_ORDER, 2, W_H)
    fwd, bwd = f4[:, :, 0], f4[:, :, 1]
    k = jnp.concatenate([fwd, jnp.zeros_like(fwd[:1]), bwd[:0:-1]], axis=0)
    k = k.reshape(n2, HY_N1, HY_ORDER, HY_NCT, HY_CT).transpose(2, 3, 0, 1, 4)
    k = k.reshape(HY_ORDER, HY_NCT, n2, HY_N1 * HY_CT)
    fh, fl = _split_bf16(fwd_t)
    full = lambda s: pl.BlockSpec(s, lambda i, j: (0, 0))
    ospec = pl.BlockSpec((1, 1, nfp, HY_N1 * HY_CT), lambda i, j: (i, j, 0, 0))
    oshape = jax.ShapeDtypeStruct((HY_ORDER, HY_NCT, nfp, HY_N1 * HY_CT), F32)
    return pl.pallas_call(
        functools.partial(_hy_kf_kernel, nfp=nfp, row_chunk=_hy_row_chunk(nfp)),
        grid=(HY_ORDER, HY_NCT),
        in_specs=[pl.BlockSpec((1, 1, n2, HY_N1 * HY_CT), lambda i, j: (i, j, 0, 0)),
                  full((2 * nfp, n2)), full((2 * nfp, n2)),
                  full((nfp, HY_N1 * HY_CT)), full((nfp, HY_N1 * HY_CT))],
        out_specs=[ospec, ospec],
        out_shape=[oshape, oshape],
        scratch_shapes=[pltpu.VMEM((2 * nfp, HY_N1 * HY_CT), F32)],
        compiler_params=_cparams(("parallel", "parallel")),
        name="hyena_filter_spectrum",
    )(k, fh, fl, tw_re, tw_im)


def _hy_conv_kernel(v_ref, x1_ref, x2_ref, kr_ref, ki_ref, bias_ref, fwd_ref, inv_ref, twr_ref, twi_ref,
                    o_ref, z_ref, u_ref, *, nfp, row_chunk):
    def long_conv(u, order):
        z_ref[...] = _dot(fwd_ref[...], u.astype(BF16))
        for c in range(nfp // row_chunk):
            rows = pl.ds(c * row_chunk, row_chunk)
            spec = _hy_spectrum(z_ref, twr_ref, twi_ref, rows, nfp)
            kf = _blocks(kr_ref.at[order, 0], ki_ref.at[order, 0], rows)
            y = _fft8([_cmul(spec[f1], kf[f1]) for f1 in range(HY_N1)], +1)
            for t1 in range(HY_N1):
                sl = slice(t1 * HY_CT, (t1 + 1) * HY_CT)
                w = _cmul(y[t1], (twr_ref[rows, sl], -twi_ref[rows, sl]))
                u_ref[rows, sl] = w[0].astype(BF16)
                u_ref[pl.ds(nfp + c * row_chunk, row_chunk), sl] = w[1].astype(BF16)
        return _dot(inv_ref[...], u_ref[...])

    v = v_ref[0, 0]
    bias = bias_ref[0]
    z1 = x1_ref[0, 0] * (long_conv(v, 0) + v * bias[0:1])
    o_ref[0, 0] = x2_ref[0, 0] * (long_conv(z1, 1) + z1 * bias[1:2])


def _hy_conv(v, x1, x2, kf_re, kf_im, bias, l, tables):
    b = v.shape[0]
    n, n2, nf, nfp = _hy_sizes(l)
    fwd_t, inv_t, tw_re, tw_im = tables
    rows = l // HY_N1
    wide = HY_N1 * HY_CT
    fold = lambda a: a.reshape(b, HY_NCT, rows, wide)
    bias_t = jnp.tile(bias.reshape(HY_ORDER, HY_NCT, 1, HY_CT), (1, 1, HY_N1, 1))
    bias_t = bias_t.transpose(1, 0, 2, 3).reshape(HY_NCT, HY_ORDER, wide)
    aspec = pl.BlockSpec((1, 1, rows, wide), lambda j, i: (i, j, 0, 0))
    kspec = pl.BlockSpec((HY_ORDER, 1, nfp, wide), lambda j, i: (0, j, 0, 0))
    full = lambda s: pl.BlockSpec(s, lambda j, i: (0, 0))
    z = pl.pallas_call(
        functools.partial(_hy_conv_kernel, nfp=nfp, row_chunk=_hy_row_chunk(nfp)),
        grid=(HY_NCT, b),
        in_specs=[aspec, aspec, aspec, kspec, kspec,
                  pl.BlockSpec((1, HY_ORDER, wide), lambda j, i: (j, 0, 0)),
                  full((2 * nfp, rows)), full((rows, 2 * nfp)), full((nfp, wide)), full((nfp, wide))],
        out_specs=aspec,
        out_shape=jax.ShapeDtypeStruct((b, HY_NCT, rows, wide), F32),
        scratch_shapes=[pltpu.VMEM((2 * nfp, wide), F32), pltpu.VMEM((2 * nfp, wide), BF16)],
        compiler_params=_cparams(("parallel", "arbitrary")),
        name="hyena_conv",
    )(fold(v), fold(x1), fold(x2), kf_re, kf_im, bias_t,
      fwd_t[:, :rows].astype(BF16), inv_t.astype(BF16), tw_re, tw_im)
    return z.reshape(b, HY_NCT, l, HY_CT)


def _merge_kernel(x_ref, pg_ref, a_ref, ys_ref, om_ref, zh_ref, gb_ref, wa_ref, wglu_ref, wo_ref,
                  wh_ref, wmix_ref, o_ref):
    d = D_MODEL
    y_a = _dot(a_ref[0].astype(BF16), wa_ref[...])
    glu = _dot(jax.nn.gelu(ys_ref[0]).astype(BF16), wglu_ref[...])
    y_s = glu[:, :d] * jax.nn.sigmoid(glu[:, d:])
    y_m = _dot(om_ref[0].astype(BF16), wo_ref[...])
    y_h = _dot(zh_ref[0, 0].astype(BF16), wh_ref[0])
    for c in range(1, HY_NCT):
        y_h = y_h + _dot(zh_ref[0, c].astype(BF16), wh_ref[c])
    gb = gb_ref[...]
    merged = jnp.zeros_like(y_a)
    for i, y in enumerate((y_a, y_s, y_m, y_h)):
        merged = merged + jax.nn.sigmoid(pg_ref[0, :, i * d:(i + 1) * d] + gb[i:i + 1]) * y
    o_ref[0] = x_ref[0] + _dot(merged.astype(BF16), wmix_ref[...])


def _merge(x, p3, a_pre, ys, o_mla, z_hy, gate_bias, w_out_a, w_glu, w_o, hy_w_out, w_mix, *, tm=256):
    b, l, d = x.shape
    row = lambda w: pl.BlockSpec((1, tm, w), lambda i, j: (i, j, 0))
    full = lambda s: pl.BlockSpec(s, lambda i, j: tuple(0 for _ in s))
    return pl.pallas_call(
        _merge_kernel,
        grid=(b, l // tm),
        in_specs=[row(d), row(N_BRANCH * d), row(W_A), row(W_S), row(MLA_HEADS * V_DIM),
                  pl.BlockSpec((1, HY_NCT, tm, HY_CT), lambda i, j: (i, 0, j, 0)),
                  full((N_BRANCH, d)), full((W_A, d)), full((W_S, 2 * d)), full((MLA_HEADS * V_DIM, d)),
                  full((HY_NCT, HY_CT, d)), full((d, d))],
        out_specs=row(d),
        out_shape=jax.ShapeDtypeStruct((b, l, d), F32),
        compiler_params=_cparams(("parallel", "parallel")),
        name="merge",
    )(x, p3, a_pre, ys, o_mla, z_hy, gate_bias, w_out_a.astype(BF16), w_glu.astype(BF16),
      w_o.astype(BF16), hy_w_out.reshape(HY_NCT, HY_CT, d).astype(BF16), w_mix.astype(BF16))


def _mem_kv_kernel(m_ref, g_ref, w_ref, o_ref):
    o_ref[0] = _dot(_rms(m_ref[0], g_ref[...]).astype(BF16), w_ref[...]).astype(BF16)


def _mem_kv(mem, mem_norm, w_kv):
    b, m, d = mem.shape
    n = w_kv.shape[1]
    return pl.pallas_call(
        _mem_kv_kernel,
        grid=(b,),
        in_specs=[pl.BlockSpec((1, m, d), lambda i: (i, 0, 0)),
                  pl.BlockSpec((1, d), lambda i: (0, 0)),
                  pl.BlockSpec((d, n), lambda i: (0, 0))],
        out_specs=pl.BlockSpec((1, m, n), lambda i: (i, 0, 0)),
        out_shape=jax.ShapeDtypeStruct((b, m, n), BF16),
        compiler_params=_cparams(("parallel",)),
        name="mem_kv",
    )(mem, mem_norm, w_kv.astype(BF16))


def _xattn_kernel(x_ref, g_ref, kv_ref, wq_ref, wo_ref, o_ref):
    x = x_ref[0]
    h = _rms(x, g_ref[...]).astype(BF16)
    q = (_dot(h, wq_ref[...]) * (XA_DH ** -0.5)).astype(BF16)
    outs = []
    for hd in range(XA_HEADS):
        k = kv_ref[0, :, hd * 2 * XA_DH:hd * 2 * XA_DH + XA_DH]
        v = kv_ref[0, :, hd * 2 * XA_DH + XA_DH:(hd + 1) * 2 * XA_DH]
        s = lax.dot_general(q[:, hd * XA_DH:(hd + 1) * XA_DH], k, (((1,), (1,)), ((), ())),
                            preferred_element_type=F32)
        e = jnp.exp(s - jnp.max(s, axis=-1, keepdims=True))
        p = (e / jnp.sum(e, axis=-1, keepdims=True)).astype(BF16)
        outs.append(_dot(p, v))
    o = jnp.concatenate(outs, axis=-1).astype(BF16)
    o_ref[0] = x + _dot(o, wo_ref[...])


def _xattn(x, kv, xa_norm, w_q, w_o, *, tm=512):
    b, l, d = x.shape
    m, n = kv.shape[1], kv.shape[2]
    full = lambda s: pl.BlockSpec(s, lambda i, j: tuple(0 for _ in s))
    return pl.pallas_call(
        _xattn_kernel,
        grid=(b, l // tm),
        in_specs=[pl.BlockSpec((1, tm, d), lambda i, j: (i, j, 0)), full((1, d)),
                  pl.BlockSpec((1, m, n), lambda i, j: (i, 0, 0)),
                  full((d, XA_HEADS * XA_DH)), full((XA_HEADS * XA_DH, d))],
        out_specs=pl.BlockSpec((1, tm, d), lambda i, j: (i, j, 0)),
        out_shape=jax.ShapeDtypeStruct((b, l, d), F32),
        compiler_params=_cparams(("parallel", "parallel")),
        name="xattn",
    )(x, xa_norm, kv, w_q.astype(BF16), w_o.astype(BF16))


def _moe_route(logits):
    neg = -jnp.inf
    big = float(1 << 20)
    lane = lax.broadcasted_iota(jnp.int32, logits.shape, 1).astype(F32)
    gl = jnp.where(lane < N_GROUPS, logits, neg)
    gmax = jnp.max(gl, axis=-1, keepdims=True)
    g_idx = jnp.min(jnp.where(gl == gmax, lane, big), axis=-1, keepdims=True)
    g_w = 1.0 / jnp.sum(jnp.exp(gl - gmax), axis=-1, keepdims=True)
    lo = N_GROUPS + g_idx * EXP_PER_GROUP
    el = jnp.where((lane >= lo) & (lane < lo + EXP_PER_GROUP), logits, neg)
    v1 = jnp.max(el, axis=-1, keepdims=True)
    i1 = jnp.min(jnp.where(el == v1, lane, big), axis=-1, keepdims=True)
    el2 = jnp.where(lane == i1, neg, el)
    v2 = jnp.max(el2, axis=-1, keepdims=True)
    i2 = jnp.min(jnp.where(el2 == v2, lane, big), axis=-1, keepdims=True)
    e2 = jnp.exp(v2 - v1)
    w1 = g_w / (1.0 + e2)
    w2 = g_w * e2 / (1.0 + e2)
    return jnp.where(lane == i1, w1, 0.0) + jnp.where(lane == i2, w2, 0.0)


def _moe_kernel(x_ref, g_ref, wrh_ref, wrl_ref, br_ref, wg_ref, wu_ref, wd_ref, fn_ref, o_ref,
                h_ref, cw_ref, acc_ref, *, final_norm):
    grp = pl.program_id(1)

    @pl.when(grp == 0)
    def _():
        h = _rms(x_ref[...], g_ref[...])
        h_hi, h_lo = _split_bf16(h)
        h_ref[...] = h_hi
        logits = _dot3(h_hi, h_lo, wrh_ref[...], wrl_ref[...]) + br_ref[...]
        cw_ref[...] = _moe_route(logits)
        acc_ref[...] = jnp.zeros_like(acc_ref)

    h = h_ref[...]
    gate = _dot(h, wg_ref[0])
    up = _dot(h, wu_ref[0])
    hid = jax.nn.silu(gate) * up
    cw = cw_ref[...]
    lane = lax.broadcasted_iota(jnp.int32, cw.shape, 1)
    parts = []
    for e in range(EXP_PER_GROUP):
        col = jnp.sum(jnp.where(lane == N_GROUPS + grp * EXP_PER_GROUP + e, cw, 0.0), axis=-1, keepdims=True)
        parts.append((hid[:, e * D_FF_E:(e + 1) * D_FF_E] * col).astype(BF16))
    acc_ref[...] += _dot(jnp.concatenate(parts, axis=-1), wd_ref[0])

    @pl.when(grp == N_GROUPS - 1)
    def _():
        y = x_ref[...] + acc_ref[...]
        if final_norm:
            y = _rms(y, fn_ref[...])
        o_ref[...] = y


def _moe(x2d, moe_norm, w_group, b_group, w_expert, b_expert, w_gate, w_up, w_down, fnorm, *,
         final_norm, tm=512):
    t, d = x2d.shape
    npad = LANE - N_GROUPS - N_EXPERTS
    wr = jnp.concatenate([w_group, w_expert, jnp.zeros((d, npad), F32)], axis=1)
    br = jnp.concatenate([b_group, b_expert, jnp.zeros((npad,), F32)])[None]
    wrh, wrl = _split_bf16(wr)
    ge = EXP_PER_GROUP * D_FF_E
    wg = w_gate.reshape(N_GROUPS, EXP_PER_GROUP, d, D_FF_E).transpose(0, 2, 1, 3).reshape(N_GROUPS, d, ge)
    wu = w_up.reshape(N_GROUPS, EXP_PER_GROUP, d, D_FF_E).transpose(0, 2, 1, 3).reshape(N_GROUPS, d, ge)
    wd = w_down.reshape(N_GROUPS, ge, d)
    full = lambda s: pl.BlockSpec(s, lambda i, j: tuple(0 for _ in s))
    return pl.pallas_call(
        functools.partial(_moe_kernel, final_norm=final_norm),
        grid=(t // tm, N_GROUPS),
        in_specs=[pl.BlockSpec((tm, d), lambda i, j: (i, 0)), full((1, d)),
                  full((d, LANE)), full((d, LANE)), full((1, LANE)),
                  pl.BlockSpec((1, d, ge), lambda i, j: (j, 0, 0)),
                  pl.BlockSpec((1, d, ge), lambda i, j: (j, 0, 0)),
                  pl.BlockSpec((1, ge, d), lambda i, j: (j, 0, 0)),
                  full((1, d))],
        out_specs=pl.BlockSpec((tm, d), lambda i, j: (i, 0)),
        out_shape=jax.ShapeDtypeStruct((t, d), F32),
        scratch_shapes=[pltpu.VMEM((tm, d), BF16), pltpu.VMEM((tm, LANE), F32), pltpu.VMEM((tm, d), F32)],
        compiler_params=_cparams(("parallel", "arbitrary")),
        name="moe",
    )(x2d, moe_norm, wrh, wrl, br, wg.astype(BF16), wu.astype(BF16), wd.astype(BF16), fnorm)


def _inproj_weight(w_in):
    d = w_in.shape[0]
    kr = w_in[:, _IN_M + Q_LORA + KV_LORA:_IN_H]
    half = QK_ROPE // 2
    kr_rot = jnp.concatenate([-kr[:, half:], kr[:, :half]], axis=1)
    z64 = jnp.zeros((d, QK_NOPE), F32)
    z32 = jnp.zeros((d, LANE - QK_NOPE - QK_ROPE), F32)
    cols = [w_in[:, _IN_G:], w_in[:, _IN_A:_IN_S], w_in[:, _IN_H:_IN_G], w_in[:, _IN_S:_IN_M],
            w_in[:, _IN_M:_IN_M + Q_LORA], w_in[:, _IN_M + Q_LORA:_IN_M + Q_LORA + KV_LORA],
            z64, kr, z32, z64, kr_rot, z32]
    return jnp.concatenate(cols, axis=1).astype(BF16)


def _rope_tables(positions):
    inv_freq = 1.0 / (ROPE_BASE ** (jnp.arange(0, QK_ROPE, 2, dtype=F32) / QK_ROPE))
    ang = positions.astype(F32)[..., None] * inv_freq
    cos, sin = jnp.cos(ang), jnp.sin(ang)
    shp = positions.shape
    pad = jnp.zeros(shp + (HEAD_PAD - QK_NOPE - QK_ROPE,), F32)
    cos_t = jnp.concatenate([jnp.ones(shp + (QK_NOPE,), F32), cos, cos, pad], axis=-1)
    sin_t = jnp.concatenate([jnp.zeros(shp + (QK_NOPE,), F32), sin, sin, pad], axis=-1)
    return cos_t, sin_t


def kernel(x, mem, positions, mix_norm, w_in, gate_bias, conv_a, w_out_a, s5_lambda_re, s5_lambda_im, s5_log_step, s5_b_re, s5_b_im, s5_c_re, s5_c_im, s5_d, s5_w_glu, mla_q_norm, mla_w_q_b, mla_kv_norm, mla_w_kv_b, mla_w_o, hy_conv, hy_f_w1, hy_f_b1, hy_f_w2, hy_f_b2, hy_f_w3, hy_f_freq, hy_bias, hy_w_out, w_mix_out, xa_norm, mem_norm, xa_w_q, xa_w_kv, xa_w_o, moe_norm, moe_w_group, moe_b_group, moe_w_expert, moe_b_expert, moe_w_gate, moe_w_up, moe_w_down, final_norm):
    b, l, d = x.shape
    depth = w_in.shape[0]
    cos_t, sin_t = _rope_tables(positions)
    tables = _hy_tables(l)
    for i in range(depth):
        p3 = _inproj(x.reshape(b * l, d), mix_norm[i][None], _inproj_weight(w_in[i])).reshape(b, l, N_P)
        a_pre, hv, hx1, hx2 = _shortconv(p3, conv_a[i], hy_conv[i])
        ys = _s5(p3, _s5_operators(s5_lambda_re[i], s5_lambda_im[i], s5_log_step[i], s5_b_re[i], s5_b_im[i],
                                   s5_c_re[i], s5_c_im[i], s5_d[i]))
        wq, wkv = _mla_weights(mla_w_q_b[i], mla_w_kv_b[i])
        q, k, v = _mla_prep(p3, cos_t, sin_t, mla_q_norm[i][None], mla_kv_norm[i][None], wq, wkv)
        o_mla = _mla_attn(q, k, v)
        filt = _hy_filter_mlp(l, hy_f_w1[i], hy_f_b1[i], hy_f_w2[i], hy_f_b2[i], hy_f_w3[i], hy_f_freq[i])
        kf_re, kf_im = _hy_filter_spectrum(filt, l, tables)
        z_hy = _hy_conv(hv, hx1, hx2, kf_re, kf_im, hy_bias[i], l, tables)
        x = _merge(x, p3, a_pre, ys, o_mla, z_hy, gate_bias[i], w_out_a[i], s5_w_glu[i], mla_w_o[i],
                   hy_w_out[i], w_mix_out[i])
        kv = _mem_kv(mem, mem_norm[None], xa_w_kv[i])
        x = _xattn(x, kv, xa_norm[i][None], xa_w_q[i], xa_w_o[i])
        x = _moe(x.reshape(b * l, d), moe_norm[i][None], moe_w_group[i], moe_b_group[i], moe_w_expert[i],
                 moe_b_expert[i], moe_w_gate[i], moe_w_up[i], moe_w_down[i], final_norm[None],
                 final_norm=(i == depth - 1)).reshape(b, l, d)
    return x
```

```python
import functools
import math

import numpy as np
import jax
import jax.numpy as jnp
from jax import lax
from jax.experimental import pallas as pl
from jax.experimental.pallas import tpu as pltpu

F32 = jnp.float32
BF16 = jnp.bfloat16
EPS = 1e-6

D_MODEL = 1024
N_BRANCH = 4
W_A = 512
W_S = 512
S5_GROUP = 16
S5_GROUPS = W_S // S5_GROUP
S5_STATE = 64
S5_CHUNK = 16
MLA_HEADS = 8
Q_LORA = 256
KV_LORA = 256
QK_NOPE = 64
QK_ROPE = 32
V_DIM = 64
ROPE_BASE = 10000.0
W_H = 512
HY_ORDER = 2
HY_EMB = 33
HY_BANDS = (HY_EMB - 1) // 2
HY_FO = 64
HY_FAST_DECAY = 0.3
HY_SLOW_DECAY = 1.5
HY_TARGET = 1e-2
XA_HEADS = 4
XA_DH = 128
N_GROUPS = 4
EXP_PER_GROUP = 8
N_EXPERTS = N_GROUPS * EXP_PER_GROUP
D_FF_E = 256

LANE = 128
HEAD_PAD = 128
VMEM_LIMIT = 56 * 1024 * 1024

OFF_G = 0
OFF_A = OFF_G + N_BRANCH * D_MODEL
OFF_H = OFF_A + 3 * W_A
OFF_S = OFF_H + 3 * W_H
OFF_CQ = OFF_S + W_S
OFF_CKV = OFF_CQ + Q_LORA
OFF_KRA = OFF_CKV + KV_LORA
OFF_KRB = OFF_KRA + LANE
N_P = OFF_KRB + LANE

_IN_A = 0
_IN_S = 3 * W_A
_IN_M = _IN_S + W_S
_IN_H = _IN_M + Q_LORA + KV_LORA + QK_ROPE
_IN_G = _IN_H + 3 * W_H

HY_N1 = 8
HY_CT = 128
HY_NCT = W_H // HY_CT


def _cparams(sem, vmem=VMEM_LIMIT):
    return pltpu.CompilerParams(dimension_semantics=sem, vmem_limit_bytes=vmem)


def _split_bf16(x):
    hi = x.astype(BF16)
    lo = (x - hi.astype(F32)).astype(BF16)
    return hi, lo


def _dot(a, b):
    return jnp.dot(a, b, preferred_element_type=F32)


def _dot3(a_hi, a_lo, b_hi, b_lo):
    return _dot(a_hi, b_hi) + (_dot(a_lo, b_hi) + _dot(a_hi, b_lo))


def _rms(x, g):
    return x * lax.rsqrt(jnp.mean(x * x, axis=-1, keepdims=True) + EPS) * g


def _inproj_kernel(x_ref, g_ref, w_ref, o_ref, h_ref):
    @pl.when(pl.program_id(1) == 0)
    def _():
        h_ref[...] = _rms(x_ref[...], g_ref[...]).astype(BF16)

    o_ref[...] = _dot(h_ref[...], w_ref[...]).astype(o_ref.dtype)


def _inproj(x2d, g, w, *, tm=1024, tn=768):
    t = x2d.shape[0]
    return pl.pallas_call(
        _inproj_kernel,
        grid=(t // tm, N_P // tn),
        in_specs=[pl.BlockSpec((tm, D_MODEL), lambda i, j: (i, 0)),
                  pl.BlockSpec((1, D_MODEL), lambda i, j: (0, 0)),
                  pl.BlockSpec((D_MODEL, tn), lambda i, j: (0, j))],
        out_specs=pl.BlockSpec((tm, tn), lambda i, j: (i, j)),
        out_shape=jax.ShapeDtypeStruct((t, N_P), F32),
        scratch_shapes=[pltpu.VMEM((tm, D_MODEL), BF16)],
        compiler_params=_cparams(("parallel", "arbitrary")),
        name="inproj",
    )(x2d, g, w)


def _conv3(u, w):
    n = u.shape[0]
    row = lax.broadcasted_iota(jnp.int32, u.shape, 0)
    prev = jnp.where(row == 0, 0.0, pltpu.roll(u, 1, axis=0))
    nxt = jnp.where(row == n - 1, 0.0, pltpu.roll(u, n - 1, axis=0))
    return w[0:1] * prev + w[1:2] * u + w[2:3] * nxt


def _shortconv_kernel(bg_ref, cg_ref, xi_ref, wa_ref, a_ref):
    a_ref[0] = bg_ref[0] * _conv3(cg_ref[0] * xi_ref[0], wa_ref[0])


def _shortconv(p3, conv_a):
    b, l, _ = p3.shape
    nct = W_A // LANE
    wa = conv_a.reshape(3, nct, LANE).transpose(1, 0, 2)

    def pspec(off):
        return pl.BlockSpec((1, l, LANE), lambda i, j, off=off: (i, 0, off // LANE + j))

    return pl.pallas_call(
        _shortconv_kernel,
        grid=(b, nct),
        in_specs=[pspec(OFF_A), pspec(OFF_A + W_A), pspec(OFF_A + 2 * W_A),
                  pl.BlockSpec((1, 3, LANE), lambda i, j: (j, 0, 0))],
        out_specs=pl.BlockSpec((1, l, LANE), lambda i, j: (i, 0, j)),
        out_shape=jax.ShapeDtypeStruct((b, l, W_A), F32),
        compiler_params=_cparams(("parallel", "parallel")),
        name="shortconv",
    )(p3, p3, p3, wa)


S5_GPB = LANE // S5_GROUP
S5_NLB = W_S // LANE
S5_XW = S5_CHUNK * LANE
S5_SW = S5_GPB * 2 * S5_STATE


def _s5_operators(lam_re, lam_im, log_step, b_re, b_im, c_re, c_im, d_skip):
    q, hh, g = S5_CHUNK, S5_GROUP, S5_GROUPS
    hp = lax.Precision.HIGHEST
    delta = jnp.exp(log_step)[..., None]

    def powers(exps):
        e = jnp.asarray(exps, F32)
        mag = jnp.exp((lam_re * delta)[..., None] * e)
        ang = (lam_im * delta)[..., None] * e
        return mag * jnp.cos(ang), mag * jnp.sin(ang)

    ramp = np.arange(q)
    p1r, p1i = powers([1.0])
    den = lam_re * lam_re + lam_im * lam_im
    nr, ni = p1r[..., 0] - 1.0, p1i[..., 0]
    fr = (nr * lam_re + ni * lam_im) / den
    fi = (ni * lam_re - nr * lam_im) / den
    bbr = fr[..., None] * b_re - fi[..., None] * b_im
    bbi = fr[..., None] * b_im + fi[..., None] * b_re

    def times_bbar(exps):
        p_r, p_i = powers(exps)
        return (p_r[..., None] * bbr[..., None, :] - p_i[..., None] * bbi[..., None, :],
                p_r[..., None] * bbi[..., None, :] + p_i[..., None] * bbr[..., None, :])

    mr, mi = times_bbar(ramp)
    kern = (jnp.einsum('dghp,dgpek->dgehk', c_re, mr, precision=hp)
            - jnp.einsum('dghp,dgpek->dgehk', c_im, mi, precision=hp))
    i_idx = jnp.arange(q)[:, None]
    j_idx = jnp.arange(q)[None, :]
    diff_f = jnp.clip(j_idx - i_idx, 0, q - 1)
    diff_b = jnp.clip(i_idx - j_idx, 0, q - 1)
    t_f = jnp.where((i_idx <= j_idx)[None, :, :, None, None], kern[0][:, diff_f], 0.0)
    t_b = jnp.where((i_idx >= j_idx)[None, :, :, None, None], kern[1][:, diff_b], 0.0)
    t_all = (t_f + t_b).transpose(0, 1, 4, 2, 3)
    skip = jnp.eye(q, dtype=F32)[:, None, :, None] * jnp.eye(hh, dtype=F32)[None, :, None, :]
    t_all = t_all + skip[None] * d_skip.reshape(g, 1, 1, 1, hh)
    eye_g = jnp.eye(S5_GPB, dtype=F32)
    blk = lambda a: a.reshape((S5_NLB, S5_GPB) + a.shape[1:])
    t_big = jnp.einsum('lgihjo,gk->lighjko', blk(t_all), eye_g).reshape(S5_NLB, S5_XW, S5_XW)

    er, ei = times_bbar(q - 1 - ramp)
    state_in = lambda f, b: jnp.concatenate([f[0].transpose(0, 2, 3, 1), b[1].transpose(0, 2, 3, 1)], axis=-1)
    st_big = lambda a: jnp.einsum('lgihs,gk->lighks', blk(a), eye_g).reshape(S5_NLB, S5_XW, S5_SW)
    w1 = jnp.concatenate([t_big, st_big(state_in(er, mr)), st_big(state_in(ei, mi))], axis=-1)

    def coef(c_r, c_i, p_r, p_i):
        cr, ci = c_r.transpose(0, 2, 1)[:, :, None, :], c_i.transpose(0, 2, 1)[:, :, None, :]
        return cr * p_r[..., None] - ci * p_i[..., None], -(cr * p_i[..., None] + ci * p_r[..., None])

    pf_r, pf_i = powers(ramp + 1)
    pb_r, pb_i = powers(q - ramp)
    f_re, f_im = coef(c_re[0], c_im[0], pf_r[0], pf_i[0])
    r_re, r_im = coef(c_re[1], c_im[1], pb_r[1], pb_i[1])
    out_big = lambda a: jnp.einsum('lgsjo,gk->lgsjko', blk(a), eye_g).reshape(S5_NLB, S5_SW, S5_XW)
    wout = jnp.concatenate([out_big(jnp.concatenate([f_re, r_re], axis=1)),
                            out_big(jnp.concatenate([f_im, r_im], axis=1))], axis=1)

    lanes = lambda a: jnp.concatenate([a[0], a[1]], axis=-1).reshape(S5_NLB, S5_SW)
    pq_r, pq_i = powers([float(q)])
    lam = jnp.stack([lanes(pq_r[..., 0]), lanes(pq_i[..., 0])], axis=1)
    return w1.astype(BF16), wout.astype(BF16), lam


def _s5_kernel(u_ref, w1_ref, wo_ref, lam_ref, y_ref, yin_ref, sre_ref, sim_ref, xre_ref, xim_ref, *, n_chunks):
    q, gpb, half = S5_CHUNK, S5_GPB, S5_STATE
    xcat = jnp.concatenate([u_ref[0, pl.ds(i, n_chunks, stride=q), :].astype(BF16) for i in range(q)], axis=-1)
    m1 = _dot(xcat, w1_ref[0])
    yin_ref[...] = m1[:, :S5_XW]
    for g in range(gpb):
        sre_ref[pl.ds(g, n_chunks, stride=gpb), :] = m1[:, S5_XW + g * LANE:S5_XW + (g + 1) * LANE]
        sim_ref[pl.ds(g, n_chunks, stride=gpb), :] = m1[:, S5_XW + S5_SW + g * LANE:S5_XW + S5_SW + (g + 1) * LANE]
    lr = lam_ref[0, 0]
    li = lam_ref[0, 1]
    fwd_lane = lax.broadcasted_iota(jnp.int32, (gpb, LANE), 1) < half

    def step(k, carry):
        xr, xi = carry
        rf = pl.multiple_of(k * gpb, gpb)
        rb = pl.multiple_of((n_chunks - 1 - k) * gpb, gpb)
        xre_ref[pl.ds(rf, gpb), 0:half] = xr[:, 0:half]
        xim_ref[pl.ds(rf, gpb), 0:half] = xi[:, 0:half]
        xre_ref[pl.ds(rb, gpb), half:LANE] = xr[:, half:LANE]
        xim_ref[pl.ds(rb, gpb), half:LANE] = xi[:, half:LANE]
        ar = jnp.where(fwd_lane, sre_ref[pl.ds(rf, gpb), :], sre_ref[pl.ds(rb, gpb), :])
        ai = jnp.where(fwd_lane, sim_ref[pl.ds(rf, gpb), :], sim_ref[pl.ds(rb, gpb), :])
        return lr * xr - li * xi + ar, lr * xi + li * xr + ai

    zero = jnp.zeros((gpb, LANE), F32)
    lax.fori_loop(0, n_chunks, step, (zero, zero))
    xs = jnp.concatenate([r[pl.ds(g, n_chunks, stride=gpb), :].astype(BF16)
                          for r in (xre_ref, xim_ref) for g in range(gpb)], axis=-1)
    y = yin_ref[...] + _dot(xs, wo_ref[0])
    for j in range(q):
        y_ref[0, pl.ds(j, n_chunks, stride=q), :] = y[:, j * LANE:(j + 1) * LANE]


def _s5(p3, ops):
    b, l, _ = p3.shape
    nc = l // S5_CHUNK
    w1, wout, lam = ops
    lam = lam.reshape(S5_NLB, 2, S5_GPB, LANE)
    once = dict(pipeline_mode=pl.Buffered(1))
    return pl.pallas_call(
        functools.partial(_s5_kernel, n_chunks=nc),
        grid=(S5_NLB, b),
        in_specs=[pl.BlockSpec((1, l, LANE), lambda j, i: (i, 0, OFF_S // LANE + j)),
                  pl.BlockSpec((1, S5_XW, S5_XW + 2 * S5_SW), lambda j, i: (j, 0, 0), **once),
                  pl.BlockSpec((1, 2 * S5_SW, S5_XW), lambda j, i: (j, 0, 0), **once),
                  pl.BlockSpec((1, 2, S5_GPB, LANE), lambda j, i: (j, 0, 0, 0))],
        out_specs=pl.BlockSpec((1, l, LANE), lambda j, i: (i, 0, j)),
        out_shape=jax.ShapeDtypeStruct((b, l, W_S), F32),
        scratch_shapes=[pltpu.VMEM((nc, S5_XW), F32)] + [pltpu.VMEM((nc * S5_GPB, LANE), F32)] * 4,
        compiler_params=_cparams(("parallel", "arbitrary")),
        name="s5_scan",
    )(p3, w1, wout, lam)


def _mla_prep_kernel(cq_ref, ckv_ref, kra_ref, krb_ref, cos_ref, sin_ref, qn_ref, kvn_ref,
                     wq_ref, wkv_ref, q_ref, k_ref, v_ref):
    cqn = _rms(cq_ref[0], qn_ref[...]).astype(BF16)
    ckvn = _rms(ckv_ref[0], kvn_ref[...]).astype(BF16)
    cos = cos_ref[0]
    sin = sin_ref[0]
    kr = kra_ref[0] * cos + krb_ref[0] * sin
    lane = lax.broadcasted_iota(jnp.int32, cos.shape, 1)
    ones_col = jnp.where(lane == V_DIM, 1.0, 0.0)
    scale = (QK_NOPE + QK_ROPE) ** -0.5
    for h in range(MLA_HEADS):
        qq = _dot(cqn, wq_ref[h])
        q_ref[0, h] = ((qq[:, :HEAD_PAD] * cos + qq[:, HEAD_PAD:] * sin) * scale).astype(BF16)
        kv = _dot(ckvn, wkv_ref[h])
        k_ref[0, h] = (kv[:, :HEAD_PAD] + kr).astype(BF16)
        v_ref[0, h] = (kv[:, HEAD_PAD:] + ones_col).astype(BF16)


def _mla_weights(w_q_b, w_kv_b):
    dq = QK_NOPE + QK_ROPE
    half = QK_ROPE // 2
    wq = w_q_b.reshape(Q_LORA, MLA_HEADS, dq).transpose(1, 0, 2)
    rope = wq[..., QK_NOPE:]
    rot = jnp.concatenate([-rope[..., half:], rope[..., :half]], axis=-1)
    zq = jnp.zeros((MLA_HEADS, Q_LORA, HEAD_PAD - dq), F32)
    zn = jnp.zeros((MLA_HEADS, Q_LORA, QK_NOPE), F32)
    wq_full = jnp.concatenate([wq, zq, zn, rot, zq], axis=-1)
    wkv = w_kv_b.reshape(KV_LORA, MLA_HEADS, QK_NOPE + V_DIM).transpose(1, 0, 2)
    zk = jnp.zeros((MLA_HEADS, KV_LORA, HEAD_PAD - QK_NOPE), F32)
    zv = jnp.zeros((MLA_HEADS, KV_LORA, HEAD_PAD - V_DIM), F32)
    wkv_full = jnp.concatenate([wkv[..., :QK_NOPE], zk, wkv[..., QK_NOPE:], zv], axis=-1)
    return wq_full.astype(BF16), wkv_full.astype(BF16)


def _mla_prep(p3, cos_t, sin_t, q_norm, kv_norm, wq, wkv, *, tl=512):
    b, l, _ = p3.shape
    hspec = pl.BlockSpec((1, MLA_HEADS, tl, HEAD_PAD), lambda i, j: (i, 0, j, 0))
    hshape = jax.ShapeDtypeStruct((b, MLA_HEADS, l, HEAD_PAD), BF16)
    tab = pl.BlockSpec((1, tl, LANE), lambda i, j: (i, j, 0))
    return pl.pallas_call(
        _mla_prep_kernel,
        grid=(b, l // tl),
        in_specs=[pl.BlockSpec((1, tl, Q_LORA), lambda i, j: (i, j, OFF_CQ // Q_LORA)),
                  pl.BlockSpec((1, tl, KV_LORA), lambda i, j: (i, j, OFF_CKV // KV_LORA)),
                  pl.BlockSpec((1, tl, LANE), lambda i, j: (i, j, OFF_KRA // LANE)),
                  pl.BlockSpec((1, tl, LANE), lambda i, j: (i, j, OFF_KRB // LANE)),
                  tab, tab,
                  pl.BlockSpec((1, Q_LORA), lambda i, j: (0, 0)),
                  pl.BlockSpec((1, KV_LORA), lambda i, j: (0, 0)),
                  pl.BlockSpec((MLA_HEADS, Q_LORA, 2 * HEAD_PAD), lambda i, j: (0, 0, 0)),
                  pl.BlockSpec((MLA_HEADS, KV_LORA, 2 * HEAD_PAD), lambda i, j: (0, 0, 0))],
        out_specs=[hspec, hspec, hspec],
        out_shape=[hshape, hshape, hshape],
        compiler_params=_cparams(("parallel", "parallel")),
        name="mla_prep",
    )(p3, p3, p3, p3, cos_t, sin_t, q_norm, kv_norm, wq, wkv)


def _mla_attn_kernel(q_ref, k_ref, v_ref, o_ref):
    outs = []
    for h in range(2):
        s = lax.dot_general(q_ref[0, h], k_ref[0, h], (((1,), (1,)), ((), ())),
                            preferred_element_type=F32)
        m = jnp.max(s, axis=-1, keepdims=True)
        p = jnp.exp(s - m).astype(BF16)
        o = _dot(p, v_ref[0, h])
        outs.append(o / o[:, V_DIM:V_DIM + 1])
    lane = lax.broadcasted_iota(jnp.int32, outs[0].shape, 1)
    o_ref[0] = jnp.where(lane < V_DIM, outs[0], pltpu.roll(outs[1], V_DIM, axis=1))


def _mla_attn(q, k, v, *, tq=256):
    b, _, l, _ = q.shape
    kvspec = pl.BlockSpec((1, 2, l, HEAD_PAD), lambda i, j, t: (i, j, 0, 0))
    return pl.pallas_call(
        _mla_attn_kernel,
        grid=(b, MLA_HEADS // 2, l // tq),
        in_specs=[pl.BlockSpec((1, 2, tq, HEAD_PAD), lambda i, j, t: (i, j, t, 0)), kvspec, kvspec],
        out_specs=pl.BlockSpec((1, tq, 2 * V_DIM), lambda i, j, t: (i, t, j)),
        out_shape=jax.ShapeDtypeStruct((b, l, MLA_HEADS * V_DIM), F32),
        compiler_params=_cparams(("parallel", "parallel", "arbitrary")),
        name="mla_attn",
    )(q, k, v)


def _hy_sizes(l):
    n = 2 * l
    n2 = n // HY_N1
    nf = n2 // 2 + 1
    nfp = ((nf + 63) // 64) * 64
    return n, n2, nf, nfp


def _hy_tables(l):
    n, n2, nf, nfp = _hy_sizes(l)
    f2 = np.arange(nfp)[:, None].astype(np.float64)
    t2 = np.arange(n2 // 2)[None, :].astype(np.float64)
    valid = (np.arange(nfp) < nf)[:, None]
    ang = 2.0 * np.pi * f2 * t2 / n2
    fwd = np.concatenate([np.where(valid, np.cos(ang), 0.0), np.where(valid, -np.sin(ang), 0.0)], axis=0)
    wgt = np.where((np.arange(nfp) == 0) | (np.arange(nfp) == nf - 1), 1.0, 2.0)[:, None] * valid / n
    inv = np.concatenate([(wgt * np.cos(ang)).T, (-wgt * np.sin(ang)).T], axis=1)
    t1 = np.arange(HY_N1)[None, :].astype(np.float64)
    tw_ang = 2.0 * np.pi * f2 * t1 / n
    tw_re = np.repeat(np.cos(tw_ang), HY_CT, axis=1).astype(np.float32)
    tw_im = np.repeat(-np.sin(tw_ang), HY_CT, axis=1).astype(np.float32)
    fwd_hi, fwd_lo = _split_bf16(jnp.asarray(fwd, F32))
    return fwd_hi, fwd_lo, jnp.asarray(inv, F32).astype(BF16), jnp.asarray(tw_re), jnp.asarray(tw_im)


def _cmul(a, b):
    return a[0] * b[0] - a[1] * b[1], a[0] * b[1] + a[1] * b[0]


def _cadd(a, b):
    return a[0] + b[0], a[1] + b[1]


def _csub(a, b):
    return a[0] - b[0], a[1] - b[1]


def _cmul_i(a, sign):
    return (-a[1], a[0]) if sign > 0 else (a[1], -a[0])


def _fft4(a, sign):
    s0, s1 = _cadd(a[0], a[2]), _csub(a[0], a[2])
    s2, s3 = _cadd(a[1], a[3]), _csub(a[1], a[3])
    r3 = _cmul_i(s3, sign)
    return [_cadd(s0, s2), _cadd(s1, r3), _csub(s0, s2), _csub(s1, r3)]


def _fft8(x, sign):
    e = _fft4([x[0], x[2], x[4], x[6]], sign)
    o = _fft4([x[1], x[3], x[5], x[7]], sign)
    r = math.sqrt(0.5)
    o1 = ((o[1][0] - sign * o[1][1]) * r, (o[1][1] + sign * o[1][0]) * r)
    o2 = _cmul_i(o[2], sign)
    o3 = ((-o[3][0] - sign * o[3][1]) * r, (-o[3][1] + sign * o[3][0]) * r)
    tw = [o[0], o1, o2, o3]
    return [_cadd(e[k], tw[k]) for k in range(4)] + [_csub(e[k], tw[k]) for k in range(4)]


def _blocks(ref_re, ref_im, rows):
    return [(ref_re[rows, k * HY_CT:(k + 1) * HY_CT], ref_im[rows, k * HY_CT:(k + 1) * HY_CT])
            for k in range(HY_N1)]


def _hy_spectrum(z_ref, twr_ref, twi_ref, rows, nfp):
    t = []
    for k in range(HY_N1):
        sl = slice(k * HY_CT, (k + 1) * HY_CT)
        zk = (z_ref[rows, sl], z_ref[pl.ds(nfp + rows.start, rows.size), sl])
        t.append(_cmul(zk, (twr_ref[rows, sl], twi_ref[rows, sl])))
    return _fft8(t, -1)


def _hy_fold(nat_ref, rows):
    return jnp.concatenate([nat_ref[pl.ds(t1, rows, stride=HY_N1), :] for t1 in range(HY_N1)], axis=-1)


def _hy_filter_kernel(z_ref, w1_ref, b1_ref, w2_ref, b2_ref, w3_ref, fr_ref, tn_ref, dl_ref,
                      fh_ref, fl_ref, twr_ref, twi_ref, kr_ref, ki_ref, ff_ref, fb_ref, zf_ref, zb_ref,
                      *, nfp, row_chunk, rows):
    hp = lax.Precision.HIGHEST
    fr = fr_ref[...]
    h = jnp.sin(fr * (jnp.dot(z_ref[...], w1_ref[...], precision=hp, preferred_element_type=F32) + b1_ref[...]))
    h = jnp.sin(fr * (jnp.dot(h, w2_ref[...], precision=hp, preferred_element_type=F32) + b2_ref[...]))
    filt = jnp.dot(h, w3_ref[0, 0], precision=hp, preferred_element_type=F32)
    decay = jnp.exp(-tn_ref[...] * dl_ref[0])
    fwd = filt[:, :HY_CT] * decay
    row = lax.broadcasted_iota(jnp.int32, fwd.shape, 0)
    bwd = jnp.where(row == 0, 0.0, filt[:, HY_CT:] * decay)
    inv = lax.rsqrt(jnp.sum(fwd * fwd, axis=0, keepdims=True) + jnp.sum(bwd * bwd, axis=0, keepdims=True) + EPS)
    ff_ref[...] = fwd
    fb_ref[...] = bwd
    for src, dst in ((ff_ref, zf_ref), (fb_ref, zb_ref)):
        k_hi, k_lo = _split_bf16(_hy_fold(src, rows))
        dst[...] = _dot3(fh_ref[...], fl_ref[...], k_hi, k_lo)
    for c in range(nfp // row_chunk):
        rws = pl.ds(c * row_chunk, row_chunk)
        sf = _hy_spectrum(zf_ref, twr_ref, twi_ref, rws, nfp)
        sb = _hy_spectrum(zb_ref, twr_ref, twi_ref, rws, nfp)
        for f1 in range(HY_N1):
            sl = slice(f1 * HY_CT, (f1 + 1) * HY_CT)
            kr_ref[0, 0, rws, sl] = (sf[f1][0] + sb[f1][0]) * inv
            ki_ref[0, 0, rws, sl] = (sf[f1][1] - sb[f1][1]) * inv


def _hy_row_chunk(nfp):
    for c in (96, 72, 64, 48, 32, 16, 8):
        if nfp % c == 0:
            return c
    return nfp


def _hy_filter_spectrum(l, w1, b1, w2, b2, w3, freq, tables):
    n, n2, nf, nfp = _hy_sizes(l)
    fwd_hi, fwd_lo, _, tw_re, tw_im = tables
    rows = l // HY_N1
    wide = HY_N1 * HY_CT
    t = np.arange(l, dtype=np.float32)
    t_norm = t / np.float32(max(l - 1, 1))
    bands = np.linspace(1e-4, HY_BANDS - 1, HY_BANDS, dtype=np.float32)
    ang = np.float32(2.0 * math.pi / l) * t[:, None] * bands[None]
    z = np.concatenate([t_norm[:, None], np.cos(ang), -np.sin(ang)], axis=-1).astype(np.float32)
    kpad = 40
    z = np.pad(z, ((0, 0), (0, kpad - HY_EMB)))
    w1p = jnp.pad(w1, ((0, kpad - HY_EMB), (0, 0)))
    max_decay = math.log(HY_TARGET) / HY_FAST_DECAY
    min_decay = math.log(HY_TARGET) / HY_SLOW_DECAY
    deltas = np.abs(np.linspace(min_decay, max_decay, W_H, dtype=np.float32)).reshape(HY_NCT, 1, HY_CT)
    w3t = w3.reshape(HY_FO, HY_ORDER, 2, HY_NCT, HY_CT).transpose(1, 3, 0, 2, 4).reshape(HY_ORDER, HY_NCT, HY_FO, 2 * HY_CT)
    full = lambda s: pl.BlockSpec(s, lambda i, j: (0, 0))
    ospec = pl.BlockSpec((1, 1, nfp, wide), lambda i, j: (i, j, 0, 0))
    oshape = jax.ShapeDtypeStruct((HY_ORDER, HY_NCT, nfp, wide), F32)
    return pl.pallas_call(
        functools.partial(_hy_filter_kernel, nfp=nfp, row_chunk=_hy_row_chunk(nfp), rows=rows),
        grid=(HY_ORDER, HY_NCT),
        in_specs=[full((l, kpad)), full((kpad, HY_FO)), full((1, HY_FO)), full((HY_FO, HY_FO)), full((1, HY_FO)),
                  pl.BlockSpec((1, 1, HY_FO, 2 * HY_CT), lambda i, j: (i, j, 0, 0)),
                  full((1, HY_FO)), full((l, 1)),
                  pl.BlockSpec((1, 1, HY_CT), lambda i, j: (j, 0, 0)),
                  full((2 * nfp, rows)), full((2 * nfp, rows)), full((nfp, wide)), full((nfp, wide))],
        out_specs=[ospec, ospec],
        out_shape=[oshape, oshape],
        scratch_shapes=[pltpu.VMEM((l, HY_CT), F32), pltpu.VMEM((l, HY_CT), F32),
                        pltpu.VMEM((2 * nfp, wide), F32), pltpu.VMEM((2 * nfp, wide), F32)],
        compiler_params=_cparams(("parallel", "parallel")),
        name="hyena_filter",
    )(jnp.asarray(z), w1p, b1[None], w2, b2[None], w3t, freq[None], jnp.asarray(t_norm[:, None]),
      jnp.asarray(deltas), fwd_hi, fwd_lo, tw_re, tw_im)


def _hy_conv_kernel(pv_ref, p1_ref, p2_ref, wc_ref, kr_ref, ki_ref, bias_ref, fwd_ref, inv_ref, twr_ref, twi_ref,
                    o_ref, nat_ref, z_ref, u_ref, *, nfp, row_chunk, rows):
    def folded_conv3(p_ref, part):
        nat_ref[...] = _conv3(p_ref[0], wc_ref[0, part])
        return _hy_fold(nat_ref, rows)

    def long_conv(u, order):
        z_ref[...] = _dot(fwd_ref[...], u.astype(BF16))
        for c in range(nfp // row_chunk):
            rows = pl.ds(c * row_chunk, row_chunk)
            spec = _hy_spectrum(z_ref, twr_ref, twi_ref, rows, nfp)
            kf = _blocks(kr_ref.at[order, 0], ki_ref.at[order, 0], rows)
            y = _fft8([_cmul(spec[f1], kf[f1]) for f1 in range(HY_N1)], +1)
            for t1 in range(HY_N1):
                sl = slice(t1 * HY_CT, (t1 + 1) * HY_CT)
                w = _cmul(y[t1], (twr_ref[rows, sl], -twi_ref[rows, sl]))
                u_ref[rows, sl] = w[0].astype(BF16)
                u_ref[pl.ds(nfp + c * row_chunk, row_chunk), sl] = w[1].astype(BF16)
        return _dot(inv_ref[...], u_ref[...])

    v = folded_conv3(pv_ref, 0)
    x1 = folded_conv3(p1_ref, 1)
    x2 = folded_conv3(p2_ref, 2)
    bias = bias_ref[0]
    z1 = x1 * (long_conv(v, 0) + v * bias[0:1])
    z2 = x2 * (long_conv(z1, 1) + z1 * bias[1:2])
    for t1 in range(HY_N1):
        o_ref[0, 0, pl.ds(t1, rows, stride=HY_N1), :] = z2[:, t1 * HY_CT:(t1 + 1) * HY_CT]


def _hy_conv(p3, hy_conv, kf_re, kf_im, bias, tables):
    b, l, _ = p3.shape
    n, n2, nf, nfp = _hy_sizes(l)
    fwd_hi, _, inv_t, tw_re, tw_im = tables
    rows = l // HY_N1
    wide = HY_N1 * HY_CT
    wc = hy_conv.reshape(3, 3, HY_NCT, HY_CT).transpose(2, 1, 0, 3)
    bias_t = jnp.tile(bias.reshape(HY_ORDER, HY_NCT, 1, HY_CT), (1, 1, HY_N1, 1))
    bias_t = bias_t.transpose(1, 0, 2, 3).reshape(HY_NCT, HY_ORDER, wide)
    once = dict(pipeline_mode=pl.Buffered(1))

    def pspec(part):
        return pl.BlockSpec((1, l, HY_CT), lambda j, i, part=part: (i, 0, (OFF_H + part * W_H) // HY_CT + j))

    kspec = pl.BlockSpec((HY_ORDER, 1, nfp, wide), lambda j, i: (0, j, 0, 0), **once)
    full = lambda s: pl.BlockSpec(s, lambda j, i: (0, 0), **once)
    return pl.pallas_call(
        functools.partial(_hy_conv_kernel, nfp=nfp, row_chunk=_hy_row_chunk(nfp), rows=rows),
        grid=(HY_NCT, b),
        in_specs=[pspec(0), pspec(1), pspec(2),
                  pl.BlockSpec((1, 3, 3, HY_CT), lambda j, i: (j, 0, 0, 0)),
                  kspec, kspec,
                  pl.BlockSpec((1, HY_ORDER, wide), lambda j, i: (j, 0, 0)),
                  full((2 * nfp, rows)), full((rows, 2 * nfp)), full((nfp, wide)), full((nfp, wide))],
        out_specs=pl.BlockSpec((1, 1, l, HY_CT), lambda j, i: (i, j, 0, 0)),
        out_shape=jax.ShapeDtypeStruct((b, HY_NCT, l, HY_CT), F32),
        scratch_shapes=[pltpu.VMEM((l, HY_CT), F32), pltpu.VMEM((2 * nfp, wide), F32),
                        pltpu.VMEM((2 * nfp, wide), BF16)],
        compiler_params=_cparams(("parallel", "arbitrary")),
        name="hyena_conv",
    )(p3, p3, p3, wc, kf_re, kf_im, bias_t, fwd_hi, inv_t, tw_re, tw_im)


def _merge_kernel(x_ref, pg_ref, a_ref, ys_ref, om_ref, zh_ref, gb_ref, wa_ref, wglu_ref, wo_ref,
                  wh_ref, wmix_ref, o_ref):
    d = D_MODEL
    y_a = _dot(a_ref[0].astype(BF16), wa_ref[...])
    glu = _dot(jax.nn.gelu(ys_ref[0]).astype(BF16), wglu_ref[...])
    y_s = glu[:, :d] * jax.nn.sigmoid(glu[:, d:])
    y_m = _dot(om_ref[0].astype(BF16), wo_ref[...])
    y_h = _dot(zh_ref[0, 0].astype(BF16), wh_ref[0])
    for c in range(1, HY_NCT):
        y_h = y_h + _dot(zh_ref[0, c].astype(BF16), wh_ref[c])
    gb = gb_ref[...]
    merged = jnp.zeros_like(y_a)
    for i, y in enumerate((y_a, y_s, y_m, y_h)):
        merged = merged + jax.nn.sigmoid(pg_ref[0, :, i * d:(i + 1) * d] + gb[i:i + 1]) * y
    o_ref[0] = x_ref[0] + _dot(merged.astype(BF16), wmix_ref[...])


def _merge(x, p3, a_pre, ys, o_mla, z_hy, gate_bias, w_out_a, w_glu, w_o, hy_w_out, w_mix, *, tm=256):
    b, l, d = x.shape
    row = lambda w: pl.BlockSpec((1, tm, w), lambda i, j: (i, j, 0))
    full = lambda s: pl.BlockSpec(s, lambda i, j: tuple(0 for _ in s))
    return pl.pallas_call(
        _merge_kernel,
        grid=(b, l // tm),
        in_specs=[row(d), row(N_BRANCH * d), row(W_A), row(W_S), row(MLA_HEADS * V_DIM),
                  pl.BlockSpec((1, HY_NCT, tm, HY_CT), lambda i, j: (i, 0, j, 0)),
                  full((N_BRANCH, d)), full((W_A, d)), full((W_S, 2 * d)), full((MLA_HEADS * V_DIM, d)),
                  full((HY_NCT, HY_CT, d)), full((d, d))],
        out_specs=row(d),
        out_shape=jax.ShapeDtypeStruct((b, l, d), F32),
        compiler_params=_cparams(("parallel", "parallel")),
        name="merge",
    )(x, p3, a_pre, ys, o_mla, z_hy, gate_bias, w_out_a.astype(BF16), w_glu.astype(BF16),
      w_o.astype(BF16), hy_w_out.reshape(HY_NCT, HY_CT, d).astype(BF16), w_mix.astype(BF16))


def _mem_kv_kernel(m_ref, g_ref, w_ref, o_ref):
    o_ref[0] = _dot(_rms(m_ref[0], g_ref[...]).astype(BF16), w_ref[...]).astype(BF16)


def _mem_kv(mem, mem_norm, w_kv):
    b, m, d = mem.shape
    n = w_kv.shape[1]
    return pl.pallas_call(
        _mem_kv_kernel,
        grid=(b,),
        in_specs=[pl.BlockSpec((1, m, d), lambda i: (i, 0, 0)),
                  pl.BlockSpec((1, d), lambda i: (0, 0)),
                  pl.BlockSpec((d, n), lambda i: (0, 0))],
        out_specs=pl.BlockSpec((1, m, n), lambda i: (i, 0, 0)),
        out_shape=jax.ShapeDtypeStruct((b, m, n), BF16),
        compiler_params=_cparams(("parallel",)),
        name="mem_kv",
    )(mem, mem_norm, w_kv.astype(BF16))


def _xattn_kernel(x_ref, g_ref, kv_ref, wq_ref, wo_ref, o_ref):
    x = x_ref[0]
    h = _rms(x, g_ref[...]).astype(BF16)
    q = (_dot(h, wq_ref[...]) * (XA_DH ** -0.5)).astype(BF16)
    outs = []
    for hd in range(XA_HEADS):
        k = kv_ref[0, :, hd * 2 * XA_DH:hd * 2 * XA_DH + XA_DH]
        v = kv_ref[0, :, hd * 2 * XA_DH + XA_DH:(hd + 1) * 2 * XA_DH]
        s = lax.dot_general(q[:, hd * XA_DH:(hd + 1) * XA_DH], k, (((1,), (1,)), ((), ())),
                            preferred_element_type=F32)
        e = jnp.exp(s - jnp.max(s, axis=-1, keepdims=True))
        p = (e / jnp.sum(e, axis=-1, keepdims=True)).astype(BF16)
        outs.append(_dot(p, v))
    o = jnp.concatenate(outs, axis=-1).astype(BF16)
    o_ref[0] = x + _dot(o, wo_ref[...])


def _xattn(x, kv, xa_norm, w_q, w_o, *, tm=512):
    b, l, d = x.shape
    m, n = kv.shape[1], kv.shape[2]
    full = lambda s: pl.BlockSpec(s, lambda i, j: tuple(0 for _ in s))
    return pl.pallas_call(
        _xattn_kernel,
        grid=(b, l // tm),
        in_specs=[pl.BlockSpec((1, tm, d), lambda i, j: (i, j, 0)), full((1, d)),
                  pl.BlockSpec((1, m, n), lambda i, j: (i, 0, 0)),
                  full((d, XA_HEADS * XA_DH)), full((XA_HEADS * XA_DH, d))],
        out_specs=pl.BlockSpec((1, tm, d), lambda i, j: (i, j, 0)),
        out_shape=jax.ShapeDtypeStruct((b, l, d), F32),
        compiler_params=_cparams(("parallel", "parallel")),
        name="xattn",
    )(x, xa_norm, kv, w_q.astype(BF16), w_o.astype(BF16))


def _moe_route(logits):
    neg = -jnp.inf
    big = float(1 << 20)
    lane = lax.broadcasted_iota(jnp.int32, logits.shape, 1).astype(F32)
    gl = jnp.where(lane < N_GROUPS, logits, neg)
    gmax = jnp.max(gl, axis=-1, keepdims=True)
    g_idx = jnp.min(jnp.where(gl == gmax, lane, big), axis=-1, keepdims=True)
    g_w = 1.0 / jnp.sum(jnp.exp(gl - gmax), axis=-1, keepdims=True)
    lo = N_GROUPS + g_idx * EXP_PER_GROUP
    el = jnp.where((lane >= lo) & (lane < lo + EXP_PER_GROUP), logits, neg)
    v1 = jnp.max(el, axis=-1, keepdims=True)
    i1 = jnp.min(jnp.where(el == v1, lane, big), axis=-1, keepdims=True)
    el2 = jnp.where(lane == i1, neg, el)
    v2 = jnp.max(el2, axis=-1, keepdims=True)
    i2 = jnp.min(jnp.where(el2 == v2, lane, big), axis=-1, keepdims=True)
    e2 = jnp.exp(v2 - v1)
    w1 = g_w / (1.0 + e2)
    w2 = g_w * e2 / (1.0 + e2)
    return jnp.where(lane == i1, w1, 0.0) + jnp.where(lane == i2, w2, 0.0)


def _moe_kernel(x_ref, g_ref, wrh_ref, wrl_ref, br_ref, wg_ref, wu_ref, wd_ref, fn_ref, o_ref,
                h_ref, cw_ref, acc_ref, *, final_norm):
    grp = pl.program_id(1)

    @pl.when(grp == 0)
    def _():
        h = _rms(x_ref[...], g_ref[...])
        h_hi, h_lo = _split_bf16(h)
        h_ref[...] = h_hi
        logits = _dot3(h_hi, h_lo, wrh_ref[...], wrl_ref[...]) + br_ref[...]
        cw_ref[...] = _moe_route(logits)
        acc_ref[...] = jnp.zeros_like(acc_ref)

    h = h_ref[...]
    gate = _dot(h, wg_ref[0])
    up = _dot(h, wu_ref[0])
    hid = jax.nn.silu(gate) * up
    cw = cw_ref[...]
    lane = lax.broadcasted_iota(jnp.int32, cw.shape, 1)
    parts = []
    for e in range(EXP_PER_GROUP):
        col = jnp.sum(jnp.where(lane == N_GROUPS + grp * EXP_PER_GROUP + e, cw, 0.0), axis=-1, keepdims=True)
        parts.append((hid[:, e * D_FF_E:(e + 1) * D_FF_E] * col).astype(BF16))
    acc_ref[...] += _dot(jnp.concatenate(parts, axis=-1), wd_ref[0])

    @pl.when(grp == N_GROUPS - 1)
    def _():
        y = x_ref[...] + acc_ref[...]
        if final_norm:
            y = _rms(y, fn_ref[...])
        o_ref[...] = y


def _moe(x2d, moe_norm, w_group, b_group, w_expert, b_expert, w_gate, w_up, w_down, fnorm, *,
         final_norm, tm=512):
    t, d = x2d.shape
    npad = LANE - N_GROUPS - N_EXPERTS
    wr = jnp.concatenate([w_group, w_expert, jnp.zeros((d, npad), F32)], axis=1)
    br = jnp.concatenate([b_group, b_expert, jnp.zeros((npad,), F32)])[None]
    wrh, wrl = _split_bf16(wr)
    ge = EXP_PER_GROUP * D_FF_E
    wg = w_gate.reshape(N_GROUPS, EXP_PER_GROUP, d, D_FF_E).transpose(0, 2, 1, 3).reshape(N_GROUPS, d, ge)
    wu = w_up.reshape(N_GROUPS, EXP_PER_GROUP, d, D_FF_E).transpose(0, 2, 1, 3).reshape(N_GROUPS, d, ge)
    wd = w_down.reshape(N_GROUPS, ge, d)
    full = lambda s: pl.BlockSpec(s, lambda i, j: tuple(0 for _ in s))
    return pl.pallas_call(
        functools.partial(_moe_kernel, final_norm=final_norm),
        grid=(t // tm, N_GROUPS),
        in_specs=[pl.BlockSpec((tm, d), lambda i, j: (i, 0)), full((1, d)),
                  full((d, LANE)), full((d, LANE)), full((1, LANE)),
                  pl.BlockSpec((1, d, ge), lambda i, j: (j, 0, 0)),
                  pl.BlockSpec((1, d, ge), lambda i, j: (j, 0, 0)),
                  pl.BlockSpec((1, ge, d), lambda i, j: (j, 0, 0)),
                  full((1, d))],
        out_specs=pl.BlockSpec((tm, d), lambda i, j: (i, 0)),
        out_shape=jax.ShapeDtypeStruct((t, d), F32),
        scratch_shapes=[pltpu.VMEM((tm, d), BF16), pltpu.VMEM((tm, LANE), F32), pltpu.VMEM((tm, d), F32)],
        compiler_params=_cparams(("parallel", "arbitrary")),
        name="moe",
    )(x2d, moe_norm, wrh, wrl, br, wg.astype(BF16), wu.astype(BF16), wd.astype(BF16), fnorm)


def _inproj_weight(w_in):
    d = w_in.shape[0]
    kr = w_in[:, _IN_M + Q_LORA + KV_LORA:_IN_H]
    half = QK_ROPE // 2
    kr_rot = jnp.concatenate([-kr[:, half:], kr[:, :half]], axis=1)
    z64 = jnp.zeros((d, QK_NOPE), F32)
    z32 = jnp.zeros((d, LANE - QK_NOPE - QK_ROPE), F32)
    cols = [w_in[:, _IN_G:], w_in[:, _IN_A:_IN_S], w_in[:, _IN_H:_IN_G], w_in[:, _IN_S:_IN_M],
            w_in[:, _IN_M:_IN_M + Q_LORA], w_in[:, _IN_M + Q_LORA:_IN_M + Q_LORA + KV_LORA],
            z64, kr, z32, z64, kr_rot, z32]
    return jnp.concatenate(cols, axis=1).astype(BF16)


def _rope_tables(positions):
    inv_freq = 1.0 / (ROPE_BASE ** (jnp.arange(0, QK_ROPE, 2, dtype=F32) / QK_ROPE))
    ang = positions.astype(F32)[..., None] * inv_freq
    cos, sin = jnp.cos(ang), jnp.sin(ang)
    shp = positions.shape
    pad = jnp.zeros(shp + (HEAD_PAD - QK_NOPE - QK_ROPE,), F32)
    cos_t = jnp.concatenate([jnp.ones(shp + (QK_NOPE,), F32), cos, cos, pad], axis=-1)
    sin_t = jnp.concatenate([jnp.zeros(shp + (QK_NOPE,), F32), sin, sin, pad], axis=-1)
    return cos_t, sin_t


def kernel(x, mem, positions, mix_norm, w_in, gate_bias, conv_a, w_out_a, s5_lambda_re, s5_lambda_im, s5_log_step, s5_b_re, s5_b_im, s5_c_re, s5_c_im, s5_d, s5_w_glu, mla_q_norm, mla_w_q_b, mla_kv_norm, mla_w_kv_b, mla_w_o, hy_conv, hy_f_w1, hy_f_b1, hy_f_w2, hy_f_b2, hy_f_w3, hy_f_freq, hy_bias, hy_w_out, w_mix_out, xa_norm, mem_norm, xa_w_q, xa_w_kv, xa_w_o, moe_norm, moe_w_group, moe_b_group, moe_w_expert, moe_b_expert, moe_w_gate, moe_w_up, moe_w_down, final_norm):
    b, l, d = x.shape
    depth = w_in.shape[0]
    cos_t, sin_t = _rope_tables(positions)
    tables = _hy_tables(l)
    for i in range(depth):
        p3 = _inproj(x.reshape(b * l, d), mix_norm[i][None], _inproj_weight(w_in[i])).reshape(b, l, N_P)
        a_pre = _shortconv(p3, conv_a[i])
        ys = _s5(p3, _s5_operators(s5_lambda_re[i], s5_lambda_im[i], s5_log_step[i], s5_b_re[i], s5_b_im[i],
                                   s5_c_re[i], s5_c_im[i], s5_d[i]))
        wq, wkv = _mla_weights(mla_w_q_b[i], mla_w_kv_b[i])
        q, k, v = _mla_prep(p3, cos_t, sin_t, mla_q_norm[i][None], mla_kv_norm[i][None], wq, wkv)
        o_mla = _mla_attn(q, k, v)
        kf_re, kf_im = _hy_filter_spectrum(l, hy_f_w1[i], hy_f_b1[i], hy_f_w2[i], hy_f_b2[i], hy_f_w3[i],
                                           hy_f_freq[i], tables)
        z_hy = _hy_conv(p3, hy_conv[i], kf_re, kf_im, hy_bias[i], tables)
        x = _merge(x, p3, a_pre, ys, o_mla, z_hy, gate_bias[i], w_out_a[i], s5_w_glu[i], mla_w_o[i],
                   hy_w_out[i], w_mix_out[i])
        kv = _mem_kv(mem, mem_norm[None], xa_w_kv[i])
        x = _xattn(x, kv, xa_norm[i][None], xa_w_q[i], xa_w_o[i])
        x = _moe(x.reshape(b * l, d), moe_norm[i][None], moe_w_group[i], moe_b_group[i], moe_w_expert[i],
                 moe_b_expert[i], moe_w_gate[i], moe_w_up[i], moe_w_down[i], final_norm[None],
                 final_norm=(i == depth - 1)).reshape(b, l, d)
    return x
```

```python
import functools
import math

import numpy as np
import jax
import jax.numpy as jnp
from jax import lax
from jax.experimental import pallas as pl
from jax.experimental.pallas import tpu as pltpu

F32 = jnp.float32
BF16 = jnp.bfloat16
EPS = 1e-6

D_MODEL = 1024
N_BRANCH = 4
W_A = 512
W_S = 512
S5_GROUP = 16
S5_GROUPS = W_S // S5_GROUP
S5_STATE = 64
S5_CHUNK = 16
MLA_HEADS = 8
Q_LORA = 256
KV_LORA = 256
QK_NOPE = 64
QK_ROPE = 32
V_DIM = 64
ROPE_BASE = 10000.0
W_H = 512
HY_ORDER = 2
HY_EMB = 33
HY_BANDS = (HY_EMB - 1) // 2
HY_FO = 64
HY_FAST_DECAY = 0.3
HY_SLOW_DECAY = 1.5
HY_TARGET = 1e-2
XA_HEADS = 4
XA_DH = 128
N_GROUPS = 4
EXP_PER_GROUP = 8
N_EXPERTS = N_GROUPS * EXP_PER_GROUP
D_FF_E = 256

LANE = 128
HEAD_PAD = 128
VMEM_LIMIT = 56 * 1024 * 1024

OFF_G = 0
OFF_A = OFF_G + N_BRANCH * D_MODEL
OFF_H = OFF_A + 3 * W_A
OFF_S = OFF_H + 3 * W_H
OFF_CQ = OFF_S + W_S
OFF_CKV = OFF_CQ + Q_LORA
OFF_KRA = OFF_CKV + KV_LORA
OFF_KRB = OFF_KRA + LANE
N_P = OFF_KRB + LANE

_IN_A = 0
_IN_S = 3 * W_A
_IN_M = _IN_S + W_S
_IN_H = _IN_M + Q_LORA + KV_LORA + QK_ROPE
_IN_G = _IN_H + 3 * W_H

HY_N1 = 8
HY_CT = 128
HY_NCT = W_H // HY_CT


def _cparams(sem, vmem=VMEM_LIMIT):
    return pltpu.CompilerParams(dimension_semantics=sem, vmem_limit_bytes=vmem)


def _split_bf16(x):
    hi = x.astype(BF16)
    lo = (x - hi.astype(F32)).astype(BF16)
    return hi, lo


def _dot(a, b):
    return jnp.dot(a, b, preferred_element_type=F32)


def _dot3(a_hi, a_lo, b_hi, b_lo):
    return _dot(a_hi, b_hi) + (_dot(a_lo, b_hi) + _dot(a_hi, b_lo))


def _rms(x, g):
    return x * lax.rsqrt(jnp.mean(x * x, axis=-1, keepdims=True) + EPS) * g


def _inproj_kernel(x_ref, g_ref, w_ref, o_ref, h_ref):
    @pl.when(pl.program_id(1) == 0)
    def _():
        h_ref[...] = _rms(x_ref[...], g_ref[...]).astype(BF16)

    o_ref[...] = _dot(h_ref[...], w_ref[...]).astype(o_ref.dtype)


def _inproj(x2d, g, w, *, tm=1024, tn=N_P // 3):
    t = x2d.shape[0]
    return pl.pallas_call(
        _inproj_kernel,
        grid=(t // tm, N_P // tn),
        in_specs=[pl.BlockSpec((tm, D_MODEL), lambda i, j: (i, 0)),
                  pl.BlockSpec((1, D_MODEL), lambda i, j: (0, 0)),
                  pl.BlockSpec((D_MODEL, tn), lambda i, j: (0, j))],
        out_specs=pl.BlockSpec((tm, tn), lambda i, j: (i, j)),
        out_shape=jax.ShapeDtypeStruct((t, N_P), BF16),
        scratch_shapes=[pltpu.VMEM((tm, D_MODEL), BF16)],
        compiler_params=_cparams(("parallel", "arbitrary")),
        name="inproj",
    )(x2d, g, w)


def _conv3(u, w):
    n = u.shape[0]
    row = lax.broadcasted_iota(jnp.int32, u.shape, 0)
    prev = jnp.where(row == 0, 0.0, pltpu.roll(u, 1, axis=0))
    nxt = jnp.where(row == n - 1, 0.0, pltpu.roll(u, n - 1, axis=0))
    return w[0:1] * prev + w[1:2] * u + w[2:3] * nxt


def _shortconv_kernel(bg_ref, cg_ref, xi_ref, wa_ref, a_ref):
    f32 = lambda r: r[0].astype(F32)
    a_ref[0] = f32(bg_ref) * _conv3(f32(cg_ref) * f32(xi_ref), wa_ref[0])


def _shortconv(p3, conv_a):
    b, l, _ = p3.shape
    nct = W_A // LANE
    wa = conv_a.reshape(3, nct, LANE).transpose(1, 0, 2)

    def pspec(off):
        return pl.BlockSpec((1, l, LANE), lambda i, j, off=off: (i, 0, off // LANE + j))

    return pl.pallas_call(
        _shortconv_kernel,
        grid=(b, nct),
        in_specs=[pspec(OFF_A), pspec(OFF_A + W_A), pspec(OFF_A + 2 * W_A),
                  pl.BlockSpec((1, 3, LANE), lambda i, j: (j, 0, 0))],
        out_specs=pl.BlockSpec((1, l, LANE), lambda i, j: (i, 0, j)),
        out_shape=jax.ShapeDtypeStruct((b, l, W_A), F32),
        compiler_params=_cparams(("parallel", "parallel")),
        name="shortconv",
    )(p3, p3, p3, wa)


S5_GPB = LANE // S5_GROUP
S5_NLB = W_S // LANE
S5_XW = S5_CHUNK * LANE
S5_SW = S5_GPB * 2 * S5_STATE


def _s5_operators(lam_re, lam_im, log_step, b_re, b_im, c_re, c_im, d_skip):
    q, hh, g = S5_CHUNK, S5_GROUP, S5_GROUPS
    hp = lax.Precision.HIGHEST
    delta = jnp.exp(log_step)[..., None]

    def powers(exps):
        e = jnp.asarray(exps, F32)
        mag = jnp.exp((lam_re * delta)[..., None] * e)
        ang = (lam_im * delta)[..., None] * e
        return mag * jnp.cos(ang), mag * jnp.sin(ang)

    ramp = np.arange(q)
    p1r, p1i = powers([1.0])
    den = lam_re * lam_re + lam_im * lam_im
    nr, ni = p1r[..., 0] - 1.0, p1i[..., 0]
    fr = (nr * lam_re + ni * lam_im) / den
    fi = (ni * lam_re - nr * lam_im) / den
    bbr = fr[..., None] * b_re - fi[..., None] * b_im
    bbi = fr[..., None] * b_im + fi[..., None] * b_re

    def times_bbar(exps):
        p_r, p_i = powers(exps)
        return (p_r[..., None] * bbr[..., None, :] - p_i[..., None] * bbi[..., None, :],
                p_r[..., None] * bbi[..., None, :] + p_i[..., None] * bbr[..., None, :])

    mr, mi = times_bbar(ramp)
    kern = (jnp.einsum('dghp,dgpek->dgehk', c_re, mr, precision=hp)
            - jnp.einsum('dghp,dgpek->dgehk', c_im, mi, precision=hp))
    i_idx = jnp.arange(q)[:, None]
    j_idx = jnp.arange(q)[None, :]
    diff_f = jnp.clip(j_idx - i_idx, 0, q - 1)
    diff_b = jnp.clip(i_idx - j_idx, 0, q - 1)
    t_f = jnp.where((i_idx <= j_idx)[None, :, :, None, None], kern[0][:, diff_f], 0.0)
    t_b = jnp.where((i_idx >= j_idx)[None, :, :, None, None], kern[1][:, diff_b], 0.0)
    t_all = (t_f + t_b).transpose(0, 1, 4, 2, 3)
    skip = jnp.eye(q, dtype=F32)[:, None, :, None] * jnp.eye(hh, dtype=F32)[None, :, None, :]
    t_all = t_all + skip[None] * d_skip.reshape(g, 1, 1, 1, hh)
    eye_g = jnp.eye(S5_GPB, dtype=BF16)
    blk = lambda a: a.astype(BF16).reshape((S5_NLB, S5_GPB) + a.shape[1:])
    r_idx = lax.broadcasted_iota(jnp.int32, (S5_GPB, q * hh, S5_XW), 1)
    n_idx = lax.broadcasted_iota(jnp.int32, (S5_GPB, q * hh, S5_XW), 2)
    g_idx = lax.broadcasted_iota(jnp.int32, (S5_GPB, q * hh, S5_XW), 0)
    spread = (n_idx == (r_idx // hh) * LANE + g_idx * hh + r_idx % hh).astype(BF16)
    cols_big = lambda a: jnp.einsum('lgrc,gcn->lgrn', blk(a), spread, preferred_element_type=BF16)
    t_big = cols_big(t_all.reshape(g, q * hh, q * hh)).reshape(S5_NLB, S5_GPB, q, hh, S5_XW)
    t_big = t_big.transpose(0, 2, 1, 3, 4).reshape(S5_NLB, S5_XW, S5_XW)

    er, ei = times_bbar(q - 1 - ramp)
    state_in = lambda f, b: jnp.concatenate([f[0].transpose(0, 2, 3, 1), b[1].transpose(0, 2, 3, 1)], axis=-1)

    def st_big(a):
        a5 = blk(a).transpose(0, 2, 1, 3, 4)
        return (a5[:, :, :, :, None, :] * eye_g[None, None, :, None, :, None]).reshape(S5_NLB, S5_XW, S5_SW)

    w1 = jnp.concatenate([t_big, st_big(state_in(er, mr)), st_big(state_in(ei, mi))], axis=-1)

    def coef(c_r, c_i, p_r, p_i):
        cr, ci = c_r.transpose(0, 2, 1)[:, :, None, :], c_i.transpose(0, 2, 1)[:, :, None, :]
        return cr * p_r[..., None] - ci * p_i[..., None], -(cr * p_i[..., None] + ci * p_r[..., None])

    pf_r, pf_i = powers(ramp + 1)
    pb_r, pb_i = powers(q - ramp)
    f_re, f_im = coef(c_re[0], c_im[0], pf_r[0], pf_i[0])
    r_re, r_im = coef(c_re[1], c_im[1], pb_r[1], pb_i[1])
    out_big = lambda a: cols_big(a.reshape(g, 2 * S5_STATE, q * hh)).reshape(S5_NLB, S5_SW, S5_XW)
    wout = jnp.concatenate([out_big(jnp.concatenate([f_re, r_re], axis=1)),
                            out_big(jnp.concatenate([f_im, r_im], axis=1))], axis=1)

    lanes = lambda a: jnp.concatenate([a[0], a[1]], axis=-1).reshape(S5_NLB, S5_SW)
    pq_r, pq_i = powers([float(q)])
    lam = jnp.stack([lanes(pq_r[..., 0]), lanes(pq_i[..., 0])], axis=1)
    return w1, wout, lam


def _s5_kernel(u_ref, w1_ref, wo_ref, lam_ref, y_ref, u32_ref, yin_ref, sre_ref, sim_ref, xre_ref, xim_ref,
               *, n_chunks):
    q, gpb, half = S5_CHUNK, S5_GPB, S5_STATE
    u32_ref[...] = u_ref[0].astype(F32)
    xcat = jnp.concatenate([u32_ref[pl.ds(i, n_chunks, stride=q), :].astype(BF16) for i in range(q)], axis=-1)
    m1 = _dot(xcat, w1_ref[0])
    yin_ref[...] = m1[:, :S5_XW]
    for g in range(gpb):
        sre_ref[pl.ds(g, n_chunks, stride=gpb), :] = m1[:, S5_XW + g * LANE:S5_XW + (g + 1) * LANE]
        sim_ref[pl.ds(g, n_chunks, stride=gpb), :] = m1[:, S5_XW + S5_SW + g * LANE:S5_XW + S5_SW + (g + 1) * LANE]
    lr = lam_ref[0, 0]
    li = lam_ref[0, 1]
    fwd_lane = lax.broadcasted_iota(jnp.int32, (gpb, LANE), 1) < half

    def step(k, carry):
        xr, xi = carry
        rf = pl.multiple_of(k * gpb, gpb)
        rb = pl.multiple_of((n_chunks - 1 - k) * gpb, gpb)
        xre_ref[pl.ds(rf, gpb), 0:half] = xr[:, 0:half]
        xim_ref[pl.ds(rf, gpb), 0:half] = xi[:, 0:half]
        xre_ref[pl.ds(rb, gpb), half:LANE] = xr[:, half:LANE]
        xim_ref[pl.ds(rb, gpb), half:LANE] = xi[:, half:LANE]
        ar = jnp.where(fwd_lane, sre_ref[pl.ds(rf, gpb), :], sre_ref[pl.ds(rb, gpb), :])
        ai = jnp.where(fwd_lane, sim_ref[pl.ds(rf, gpb), :], sim_ref[pl.ds(rb, gpb), :])
        return lr * xr - li * xi + ar, lr * xi + li * xr + ai

    zero = jnp.zeros((gpb, LANE), F32)
    lax.fori_loop(0, n_chunks, step, (zero, zero))
    xs = jnp.concatenate([r[pl.ds(g, n_chunks, stride=gpb), :].astype(BF16)
                          for r in (xre_ref, xim_ref) for g in range(gpb)], axis=-1)
    y = yin_ref[...] + _dot(xs, wo_ref[0])
    for j in range(q):
        y_ref[0, pl.ds(j, n_chunks, stride=q), :] = y[:, j * LANE:(j + 1) * LANE]


def _s5(p3, ops):
    b, l, _ = p3.shape
    nc = l // S5_CHUNK
    w1, wout, lam = ops
    lam = lam.reshape(S5_NLB, 2, S5_GPB, LANE)
    once = dict(pipeline_mode=pl.Buffered(1))
    return pl.pallas_call(
        functools.partial(_s5_kernel, n_chunks=nc),
        grid=(S5_NLB, b),
        in_specs=[pl.BlockSpec((1, l, LANE), lambda j, i: (i, 0, OFF_S // LANE + j)),
                  pl.BlockSpec((1, S5_XW, S5_XW + 2 * S5_SW), lambda j, i: (j, 0, 0), **once),
                  pl.BlockSpec((1, 2 * S5_SW, S5_XW), lambda j, i: (j, 0, 0), **once),
                  pl.BlockSpec((1, 2, S5_GPB, LANE), lambda j, i: (j, 0, 0, 0))],
        out_specs=pl.BlockSpec((1, l, LANE), lambda j, i: (i, 0, j)),
        out_shape=jax.ShapeDtypeStruct((b, l, W_S), F32),
        scratch_shapes=[pltpu.VMEM((l, LANE), F32), pltpu.VMEM((nc, S5_XW), F32)]
        + [pltpu.VMEM((nc * S5_GPB, LANE), F32)] * 4,
        compiler_params=_cparams(("parallel", "arbitrary")),
        name="s5_scan",
    )(p3, w1, wout, lam)


def _mla_prep_kernel(cq_ref, ckv_ref, kra_ref, krb_ref, cos_ref, sin_ref, qn_ref, kvn_ref,
                     wq_ref, wkv_ref, q_ref, k_ref, v_ref):
    cqn = _rms(cq_ref[0].astype(F32), qn_ref[...]).astype(BF16)
    ckvn = _rms(ckv_ref[0].astype(F32), kvn_ref[...]).astype(BF16)
    cos = cos_ref[0]
    sin = sin_ref[0]
    kr = kra_ref[0].astype(F32) * cos + krb_ref[0].astype(F32) * sin
    lane = lax.broadcasted_iota(jnp.int32, cos.shape, 1)
    ones_col = jnp.where(lane == V_DIM, 1.0, 0.0)
    scale = (QK_NOPE + QK_ROPE) ** -0.5
    for h in range(MLA_HEADS):
        qq = _dot(cqn, wq_ref[h])
        q_ref[0, h] = ((qq[:, :HEAD_PAD] * cos + qq[:, HEAD_PAD:] * sin) * scale).astype(BF16)
        kv = _dot(ckvn, wkv_ref[h])
        k_ref[0, h] = (kv[:, :HEAD_PAD] + kr).astype(BF16)
        v_ref[0, h] = (kv[:, HEAD_PAD:] + ones_col).astype(BF16)


def _mla_weights(w_q_b, w_kv_b):
    dq = QK_NOPE + QK_ROPE
    half = QK_ROPE // 2
    wq = w_q_b.reshape(Q_LORA, MLA_HEADS, dq).transpose(1, 0, 2)
    rope = wq[..., QK_NOPE:]
    rot = jnp.concatenate([-rope[..., half:], rope[..., :half]], axis=-1)
    zq = jnp.zeros((MLA_HEADS, Q_LORA, HEAD_PAD - dq), F32)
    zn = jnp.zeros((MLA_HEADS, Q_LORA, QK_NOPE), F32)
    wq_full = jnp.concatenate([wq, zq, zn, rot, zq], axis=-1)
    wkv = w_kv_b.reshape(KV_LORA, MLA_HEADS, QK_NOPE + V_DIM).transpose(1, 0, 2)
    zk = jnp.zeros((MLA_HEADS, KV_LORA, HEAD_PAD - QK_NOPE), F32)
    zv = jnp.zeros((MLA_HEADS, KV_LORA, HEAD_PAD - V_DIM), F32)
    wkv_full = jnp.concatenate([wkv[..., :QK_NOPE], zk, wkv[..., QK_NOPE:], zv], axis=-1)
    return wq_full.astype(BF16), wkv_full.astype(BF16)


def _mla_prep(p3, cos_t, sin_t, q_norm, kv_norm, wq, wkv, *, tl=512):
    b, l, _ = p3.shape
    hspec = pl.BlockSpec((1, MLA_HEADS, tl, HEAD_PAD), lambda i, j: (i, 0, j, 0))
    hshape = jax.ShapeDtypeStruct((b, MLA_HEADS, l, HEAD_PAD), BF16)
    tab = pl.BlockSpec((1, tl, LANE), lambda i, j: (i, j, 0))
    return pl.pallas_call(
        _mla_prep_kernel,
        grid=(b, l // tl),
        in_specs=[pl.BlockSpec((1, tl, Q_LORA), lambda i, j: (i, j, OFF_CQ // Q_LORA)),
                  pl.BlockSpec((1, tl, KV_LORA), lambda i, j: (i, j, OFF_CKV // KV_LORA)),
                  pl.BlockSpec((1, tl, LANE), lambda i, j: (i, j, OFF_KRA // LANE)),
                  pl.BlockSpec((1, tl, LANE), lambda i, j: (i, j, OFF_KRB // LANE)),
                  tab, tab,
                  pl.BlockSpec((1, Q_LORA), lambda i, j: (0, 0)),
                  pl.BlockSpec((1, KV_LORA), lambda i, j: (0, 0)),
                  pl.BlockSpec((MLA_HEADS, Q_LORA, 2 * HEAD_PAD), lambda i, j: (0, 0, 0)),
                  pl.BlockSpec((MLA_HEADS, KV_LORA, 2 * HEAD_PAD), lambda i, j: (0, 0, 0))],
        out_specs=[hspec, hspec, hspec],
        out_shape=[hshape, hshape, hshape],
        compiler_params=_cparams(("parallel", "parallel")),
        name="mla_prep",
    )(p3, p3, p3, p3, cos_t, sin_t, q_norm, kv_norm, wq, wkv)


def _mla_attn_kernel(q_ref, k_ref, v_ref, o_ref):
    outs = []
    for h in range(2):
        s = lax.dot_general(q_ref[0, h], k_ref[0, h], (((1,), (1,)), ((), ())),
                            preferred_element_type=F32)
        m = jnp.max(s, axis=-1, keepdims=True)
        p = jnp.exp(s - m).astype(BF16)
        o = _dot(p, v_ref[0, h])
        outs.append(o / o[:, V_DIM:V_DIM + 1])
    lane = lax.broadcasted_iota(jnp.int32, outs[0].shape, 1)
    o_ref[0] = jnp.where(lane < V_DIM, outs[0], pltpu.roll(outs[1], V_DIM, axis=1))


def _mla_attn(q, k, v, *, tq=512):
    b, _, l, _ = q.shape
    kvspec = pl.BlockSpec((1, 2, l, HEAD_PAD), lambda i, j, t: (i, j, 0, 0))
    return pl.pallas_call(
        _mla_attn_kernel,
        grid=(b, MLA_HEADS // 2, l // tq),
        in_specs=[pl.BlockSpec((1, 2, tq, HEAD_PAD), lambda i, j, t: (i, j, t, 0)), kvspec, kvspec],
        out_specs=pl.BlockSpec((1, tq, 2 * V_DIM), lambda i, j, t: (i, t, j)),
        out_shape=jax.ShapeDtypeStruct((b, l, MLA_HEADS * V_DIM), F32),
        compiler_params=_cparams(("parallel", "parallel", "arbitrary")),
        name="mla_attn",
    )(q, k, v)


def _hy_sizes(l):
    n = 2 * l
    n2 = n // HY_N1
    nf = n2 // 2 + 1
    nfp = ((nf + 63) // 64) * 64
    return n, n2, nf, nfp


def _hy_tables(l):
    n, n2, nf, nfp = _hy_sizes(l)
    f2 = np.arange(nfp)[:, None].astype(np.float64)
    t2 = np.arange(n2 // 2)[None, :].astype(np.float64)
    valid = (np.arange(nfp) < nf)[:, None]
    ang = 2.0 * np.pi * f2 * t2 / n2
    fwd = np.concatenate([np.where(valid, np.cos(ang), 0.0), np.where(valid, -np.sin(ang), 0.0)], axis=0)
    wgt = np.where((np.arange(nfp) == 0) | (np.arange(nfp) == nf - 1), 1.0, 2.0)[:, None] * valid / n
    inv = np.concatenate([(wgt * np.cos(ang)).T, (-wgt * np.sin(ang)).T], axis=1)
    t1 = np.arange(HY_N1)[None, :].astype(np.float64)
    tw_ang = 2.0 * np.pi * f2 * t1 / n
    tw_re = np.repeat(np.cos(tw_ang), HY_CT, axis=1).astype(np.float32)
    tw_im = np.repeat(-np.sin(tw_ang), HY_CT, axis=1).astype(np.float32)
    fwd_hi, fwd_lo = _split_bf16(jnp.asarray(fwd, F32))
    return fwd_hi, fwd_lo, jnp.asarray(inv, F32).astype(BF16), jnp.asarray(tw_re), jnp.asarray(tw_im)


def _cmul(a, b):
    return a[0] * b[0] - a[1] * b[1], a[0] * b[1] + a[1] * b[0]


def _cadd(a, b):
    return a[0] + b[0], a[1] + b[1]


def _csub(a, b):
    return a[0] - b[0], a[1] - b[1]


def _cmul_i(a, sign):
    return (-a[1], a[0]) if sign > 0 else (a[1], -a[0])


def _fft4(a, sign):
    s0, s1 = _cadd(a[0], a[2]), _csub(a[0], a[2])
    s2, s3 = _cadd(a[1], a[3]), _csub(a[1], a[3])
    r3 = _cmul_i(s3, sign)
    return [_cadd(s0, s2), _cadd(s1, r3), _csub(s0, s2), _csub(s1, r3)]


def _fft8(x, sign):
    e = _fft4([x[0], x[2], x[4], x[6]], sign)
    o = _fft4([x[1], x[3], x[5], x[7]], sign)
    r = math.sqrt(0.5)
    o1 = ((o[1][0] - sign * o[1][1]) * r, (o[1][1] + sign * o[1][0]) * r)
    o2 = _cmul_i(o[2], sign)
    o3 = ((-o[3][0] - sign * o[3][1]) * r, (-o[3][1] + sign * o[3][0]) * r)
    tw = [o[0], o1, o2, o3]
    return [_cadd(e[k], tw[k]) for k in range(4)] + [_csub(e[k], tw[k]) for k in range(4)]


def _blocks(ref_re, ref_im, rows):
    return [(ref_re[rows, k * HY_CT:(k + 1) * HY_CT], ref_im[rows, k * HY_CT:(k + 1) * HY_CT])
            for k in range(HY_N1)]


def _hy_spectrum(z_ref, twr_ref, twi_ref, rows, nfp):
    t = []
    for k in range(HY_N1):
        sl = slice(k * HY_CT, (k + 1) * HY_CT)
        zk = (z_ref[rows, sl], z_ref[pl.ds(nfp + rows.start, rows.size), sl])
        t.append(_cmul(zk, (twr_ref[rows, sl], twi_ref[rows, sl])))
    return _fft8(t, -1)


def _hy_fold(nat_ref, rows):
    return jnp.concatenate([nat_ref[pl.ds(t1, rows, stride=HY_N1), :] for t1 in range(HY_N1)], axis=-1)


def _hy_filter_kernel(z_ref, w1_ref, b1_ref, w2_ref, b2_ref, w3_ref, fr_ref, tn_ref, dl_ref,
                      fh_ref, fl_ref, twr_ref, twi_ref, kr_ref, ki_ref, ff_ref, fb_ref, zf_ref, zb_ref,
                      *, nfp, row_chunk, rows):
    hp = lax.Precision.HIGHEST
    fr = fr_ref[...]
    h = jnp.sin(fr * (jnp.dot(z_ref[...], w1_ref[...], precision=hp, preferred_element_type=F32) + b1_ref[...]))
    h = jnp.sin(fr * (jnp.dot(h, w2_ref[...], precision=hp, preferred_element_type=F32) + b2_ref[...]))
    filt = jnp.dot(h, w3_ref[0, 0], precision=hp, preferred_element_type=F32)
    decay = jnp.exp(-tn_ref[...] * dl_ref[0])
    fwd = filt[:, :HY_CT] * decay
    row = lax.broadcasted_iota(jnp.int32, fwd.shape, 0)
    bwd = jnp.where(row == 0, 0.0, filt[:, HY_CT:] * decay)
    inv = lax.rsqrt(jnp.sum(fwd * fwd, axis=0, keepdims=True) + jnp.sum(bwd * bwd, axis=0, keepdims=True) + EPS)
    ff_ref[...] = fwd
    fb_ref[...] = bwd
    for src, dst in ((ff_ref, zf_ref), (fb_ref, zb_ref)):
        k_hi, k_lo = _split_bf16(_hy_fold(src, rows))
        dst[...] = _dot3(fh_ref[...], fl_ref[...], k_hi, k_lo)
    for c in range(nfp // row_chunk):
        rws = pl.ds(c * row_chunk, row_chunk)
        sf = _hy_spectrum(zf_ref, twr_ref, twi_ref, rws, nfp)
        sb = _hy_spectrum(zb_ref, twr_ref, twi_ref, rws, nfp)
        for f1 in range(HY_N1):
            sl = slice(f1 * HY_CT, (f1 + 1) * HY_CT)
            kr_ref[0, 0, rws, sl] = (sf[f1][0] + sb[f1][0]) * inv
            ki_ref[0, 0, rws, sl] = (sf[f1][1] - sb[f1][1]) * inv


def _hy_row_chunk(nfp):
    for c in (96, 72, 64, 48, 32, 16, 8):
        if nfp % c == 0:
            return c
    return nfp


def _hy_filter_spectrum(l, w1, b1, w2, b2, w3, freq, tables):
    n, n2, nf, nfp = _hy_sizes(l)
    fwd_hi, fwd_lo, _, tw_re, tw_im = tables
    rows = l // HY_N1
    wide = HY_N1 * HY_CT
    t = np.arange(l, dtype=np.float32)
    t_norm = t / np.float32(max(l - 1, 1))
    bands = np.linspace(1e-4, HY_BANDS - 1, HY_BANDS, dtype=np.float32)
    ang = np.float32(2.0 * math.pi / l) * t[:, None] * bands[None]
    z = np.concatenate([t_norm[:, None], np.cos(ang), -np.sin(ang)], axis=-1).astype(np.float32)
    kpad = 40
    z = np.pad(z, ((0, 0), (0, kpad - HY_EMB)))
    w1p = jnp.pad(w1, ((0, kpad - HY_EMB), (0, 0)))
    max_decay = math.log(HY_TARGET) / HY_FAST_DECAY
    min_decay = math.log(HY_TARGET) / HY_SLOW_DECAY
    deltas = np.abs(np.linspace(min_decay, max_decay, W_H, dtype=np.float32)).reshape(HY_NCT, 1, HY_CT)
    w3t = w3.reshape(HY_FO, HY_ORDER, 2, HY_NCT, HY_CT).transpose(1, 3, 0, 2, 4).reshape(HY_ORDER, HY_NCT, HY_FO, 2 * HY_CT)
    full = lambda s: pl.BlockSpec(s, lambda i, j: (0, 0))
    ospec = pl.BlockSpec((1, 1, nfp, wide), lambda i, j: (i, j, 0, 0))
    oshape = jax.ShapeDtypeStruct((HY_ORDER, HY_NCT, nfp, wide), F32)
    return pl.pallas_call(
        functools.partial(_hy_filter_kernel, nfp=nfp, row_chunk=_hy_row_chunk(nfp), rows=rows),
        grid=(HY_ORDER, HY_NCT),
        in_specs=[full((l, kpad)), full((kpad, HY_FO)), full((1, HY_FO)), full((HY_FO, HY_FO)), full((1, HY_FO)),
                  pl.BlockSpec((1, 1, HY_FO, 2 * HY_CT), lambda i, j: (i, j, 0, 0)),
                  full((1, HY_FO)), full((l, 1)),
                  pl.BlockSpec((1, 1, HY_CT), lambda i, j: (j, 0, 0)),
                  full((2 * nfp, rows)), full((2 * nfp, rows)), full((nfp, wide)), full((nfp, wide))],
        out_specs=[ospec, ospec],
        out_shape=[oshape, oshape],
        scratch_shapes=[pltpu.VMEM((l, HY_CT), F32), pltpu.VMEM((l, HY_CT), F32),
                        pltpu.VMEM((2 * nfp, wide), F32), pltpu.VMEM((2 * nfp, wide), F32)],
        compiler_params=_cparams(("parallel", "parallel")),
        name="hyena_filter",
    )(jnp.asarray(z), w1p, b1[None], w2, b2[None], w3t, freq[None], jnp.asarray(t_norm[:, None]),
      jnp.asarray(deltas), fwd_hi, fwd_lo, tw_re, tw_im)


def _hy_conv_kernel(pv_ref, p1_ref, p2_ref, wc_ref, kr_ref, ki_ref, bias_ref, fwd_ref, inv_ref, twr_ref, twi_ref,
                    o_ref, nat_ref, z_ref, u_ref, *, nfp, row_chunk, rows):
    def folded_conv3(p_ref, part):
        nat_ref[...] = _conv3(p_ref[0].astype(F32), wc_ref[0, part])
        return _hy_fold(nat_ref, rows)

    def long_conv(u, order):
        z_ref[...] = _dot(fwd_ref[...], u.astype(BF16))
        for c in range(nfp // row_chunk):
            rows = pl.ds(c * row_chunk, row_chunk)
            spec = _hy_spectrum(z_ref, twr_ref, twi_ref, rows, nfp)
            kf = _blocks(kr_ref.at[order, 0], ki_ref.at[order, 0], rows)
            y = _fft8([_cmul(spec[f1], kf[f1]) for f1 in range(HY_N1)], +1)
            for t1 in range(HY_N1):
                sl = slice(t1 * HY_CT, (t1 + 1) * HY_CT)
                w = _cmul(y[t1], (twr_ref[rows, sl], -twi_ref[rows, sl]))
                u_ref[rows, sl] = w[0].astype(BF16)
                u_ref[pl.ds(nfp + c * row_chunk, row_chunk), sl] = w[1].astype(BF16)
        return _dot(inv_ref[...], u_ref[...])

    v = folded_conv3(pv_ref, 0)
    x1 = folded_conv3(p1_ref, 1)
    x2 = folded_conv3(p2_ref, 2)
    bias = bias_ref[0]
    z1 = x1 * (long_conv(v, 0) + v * bias[0:1])
    z2 = x2 * (long_conv(z1, 1) + z1 * bias[1:2])
    for t1 in range(HY_N1):
        o_ref[0, 0, pl.ds(t1, rows, stride=HY_N1), :] = z2[:, t1 * HY_CT:(t1 + 1) * HY_CT]


def _hy_conv(p3, hy_conv, kf_re, kf_im, bias, tables):
    b, l, _ = p3.shape
    n, n2, nf, nfp = _hy_sizes(l)
    fwd_hi, _, inv_t, tw_re, tw_im = tables
    rows = l // HY_N1
    wide = HY_N1 * HY_CT
    wc = hy_conv.reshape(3, 3, HY_NCT, HY_CT).transpose(2, 1, 0, 3)
    bias_t = jnp.tile(bias.reshape(HY_ORDER, HY_NCT, 1, HY_CT), (1, 1, HY_N1, 1))
    bias_t = bias_t.transpose(1, 0, 2, 3).reshape(HY_NCT, HY_ORDER, wide)
    once = dict(pipeline_mode=pl.Buffered(1))

    def pspec(part):
        return pl.BlockSpec((1, l, HY_CT), lambda j, i, part=part: (i, 0, (OFF_H + part * W_H) // HY_CT + j))

    kspec = pl.BlockSpec((HY_ORDER, 1, nfp, wide), lambda j, i: (0, j, 0, 0), **once)
    full = lambda s: pl.BlockSpec(s, lambda j, i: (0, 0), **once)
    return pl.pallas_call(
        functools.partial(_hy_conv_kernel, nfp=nfp, row_chunk=_hy_row_chunk(nfp), rows=rows),
        grid=(HY_NCT, b),
        in_specs=[pspec(0), pspec(1), pspec(2),
                  pl.BlockSpec((1, 3, 3, HY_CT), lambda j, i: (j, 0, 0, 0)),
                  kspec, kspec,
                  pl.BlockSpec((1, HY_ORDER, wide), lambda j, i: (j, 0, 0)),
                  full((2 * nfp, rows)), full((rows, 2 * nfp)), full((nfp, wide)), full((nfp, wide))],
        out_specs=pl.BlockSpec((1, 1, l, HY_CT), lambda j, i: (i, j, 0, 0)),
        out_shape=jax.ShapeDtypeStruct((b, HY_NCT, l, HY_CT), F32),
        scratch_shapes=[pltpu.VMEM((l, HY_CT), F32), pltpu.VMEM((2 * nfp, wide), F32),
                        pltpu.VMEM((2 * nfp, wide), BF16)],
        compiler_params=_cparams(("parallel", "arbitrary")),
        name="hyena_conv",
    )(p3, p3, p3, wc, kf_re, kf_im, bias_t, fwd_hi, inv_t, tw_re, tw_im)


def _merge_kernel(x_ref, pg_ref, a_ref, ys_ref, om_ref, zh_ref, gb_ref, wa_ref, wglu_ref, wo_ref,
                  wh_ref, wmix_ref, o_ref):
    d = D_MODEL
    y_a = _dot(a_ref[0].astype(BF16), wa_ref[...])
    glu = _dot(jax.nn.gelu(ys_ref[0]).astype(BF16), wglu_ref[...])
    y_s = glu[:, :d] * jax.nn.sigmoid(glu[:, d:])
    y_m = _dot(om_ref[0].astype(BF16), wo_ref[...])
    y_h = _dot(zh_ref[0, 0].astype(BF16), wh_ref[0])
    for c in range(1, HY_NCT):
        y_h = y_h + _dot(zh_ref[0, c].astype(BF16), wh_ref[c])
    gb = gb_ref[...]
    merged = jnp.zeros_like(y_a)
    for i, y in enumerate((y_a, y_s, y_m, y_h)):
        merged = merged + jax.nn.sigmoid(pg_ref[0, :, i * d:(i + 1) * d].astype(F32) + gb[i:i + 1]) * y
    o_ref[0] = x_ref[0] + _dot(merged.astype(BF16), wmix_ref[...])


def _merge(x, p3, a_pre, ys, o_mla, z_hy, gate_bias, w_out_a, w_glu, w_o, hy_w_out, w_mix, *, tm=512):
    b, l, d = x.shape
    row = lambda w: pl.BlockSpec((1, tm, w), lambda i, j: (i, j, 0))
    full = lambda s: pl.BlockSpec(s, lambda i, j: tuple(0 for _ in s))
    return pl.pallas_call(
        _merge_kernel,
        grid=(b, l // tm),
        in_specs=[row(d), row(N_BRANCH * d), row(W_A), row(W_S), row(MLA_HEADS * V_DIM),
                  pl.BlockSpec((1, HY_NCT, tm, HY_CT), lambda i, j: (i, 0, j, 0)),
                  full((N_BRANCH, d)), full((W_A, d)), full((W_S, 2 * d)), full((MLA_HEADS * V_DIM, d)),
                  full((HY_NCT, HY_CT, d)), full((d, d))],
        out_specs=row(d),
        out_shape=jax.ShapeDtypeStruct((b, l, d), F32),
        compiler_params=_cparams(("parallel", "parallel")),
        name="merge",
    )(x, p3, a_pre, ys, o_mla, z_hy, gate_bias, w_out_a.astype(BF16), w_glu.astype(BF16),
      w_o.astype(BF16), hy_w_out.reshape(HY_NCT, HY_CT, d).astype(BF16), w_mix.astype(BF16))


def _mem_kv_kernel(m_ref, g_ref, w_ref, o_ref):
    o_ref[0] = _dot(_rms(m_ref[0], g_ref[...]).astype(BF16), w_ref[...]).astype(BF16)


def _mem_kv(mem, mem_norm, w_kv):
    b, m, d = mem.shape
    n = w_kv.shape[1]
    return pl.pallas_call(
        _mem_kv_kernel,
        grid=(b,),
        in_specs=[pl.BlockSpec((1, m, d), lambda i: (i, 0, 0)),
                  pl.BlockSpec((1, d), lambda i: (0, 0)),
                  pl.BlockSpec((d, n), lambda i: (0, 0))],
        out_specs=pl.BlockSpec((1, m, n), lambda i: (i, 0, 0)),
        out_shape=jax.ShapeDtypeStruct((b, m, n), BF16),
        compiler_params=_cparams(("parallel",)),
        name="mem_kv",
    )(mem, mem_norm, w_kv.astype(BF16))


def _xattn_kernel(x_ref, g_ref, kv_ref, wq_ref, wo_ref, o_ref):
    x = x_ref[0]
    h = _rms(x, g_ref[...]).astype(BF16)
    q = (_dot(h, wq_ref[...]) * (XA_DH ** -0.5)).astype(BF16)
    outs = []
    for hd in range(XA_HEADS):
        k = kv_ref[0, :, hd * 2 * XA_DH:hd * 2 * XA_DH + XA_DH]
        v = kv_ref[0, :, hd * 2 * XA_DH + XA_DH:(hd + 1) * 2 * XA_DH]
        s = lax.dot_general(q[:, hd * XA_DH:(hd + 1) * XA_DH], k, (((1,), (1,)), ((), ())),
                            preferred_element_type=F32)
        e = jnp.exp(s - jnp.max(s, axis=-1, keepdims=True))
        p = (e / jnp.sum(e, axis=-1, keepdims=True)).astype(BF16)
        outs.append(_dot(p, v))
    o = jnp.concatenate(outs, axis=-1).astype(BF16)
    o_ref[0] = x + _dot(o, wo_ref[...])


def _xattn(x, kv, xa_norm, w_q, w_o, *, tm=512):
    b, l, d = x.shape
    m, n = kv.shape[1], kv.shape[2]
    full = lambda s: pl.BlockSpec(s, lambda i, j: tuple(0 for _ in s))
    return pl.pallas_call(
        _xattn_kernel,
        grid=(b, l // tm),
        in_specs=[pl.BlockSpec((1, tm, d), lambda i, j: (i, j, 0)), full((1, d)),
                  pl.BlockSpec((1, m, n), lambda i, j: (i, 0, 0)),
                  full((d, XA_HEADS * XA_DH)), full((XA_HEADS * XA_DH, d))],
        out_specs=pl.BlockSpec((1, tm, d), lambda i, j: (i, j, 0)),
        out_shape=jax.ShapeDtypeStruct((b, l, d), F32),
        compiler_params=_cparams(("parallel", "parallel")),
        name="xattn",
    )(x, xa_norm, kv, w_q.astype(BF16), w_o.astype(BF16))


def _moe_route(logits):
    neg = -jnp.inf
    big = float(1 << 20)
    lane = lax.broadcasted_iota(jnp.int32, logits.shape, 1).astype(F32)
    gl = jnp.where(lane < N_GROUPS, logits, neg)
    gmax = jnp.max(gl, axis=-1, keepdims=True)
    g_idx = jnp.min(jnp.where(gl == gmax, lane, big), axis=-1, keepdims=True)
    g_w = 1.0 / jnp.sum(jnp.exp(gl - gmax), axis=-1, keepdims=True)
    lo = N_GROUPS + g_idx * EXP_PER_GROUP
    el = jnp.where((lane >= lo) & (lane < lo + EXP_PER_GROUP), logits, neg)
    v1 = jnp.max(el, axis=-1, keepdims=True)
    i1 = jnp.min(jnp.where(el == v1, lane, big), axis=-1, keepdims=True)
    el2 = jnp.where(lane == i1, neg, el)
    v2 = jnp.max(el2, axis=-1, keepdims=True)
    i2 = jnp.min(jnp.where(el2 == v2, lane, big), axis=-1, keepdims=True)
    e2 = jnp.exp(v2 - v1)
    w1 = g_w / (1.0 + e2)
    w2 = g_w * e2 / (1.0 + e2)
    return jnp.where(lane == i1, w1, 0.0) + jnp.where(lane == i2, w2, 0.0), g_idx


MOE_TM = 1024
MOE_RB = 256


def _moe_route_kernel(x_ref, g_ref, wrh_ref, wrl_ref, br_ref, tri_ref, hs_ref, cws_ref, pos_ref, seg_ref):
    tm = x_ref.shape[0]
    h = _rms(x_ref[...], g_ref[...])
    h_hi, h_lo = _split_bf16(h)
    logits = _dot3(h_hi, h_lo, wrh_ref[...], wrl_ref[...]) + br_ref[...]
    cw, g_idx = _moe_route(logits)
    lane = lax.broadcasted_iota(jnp.int32, (tm, LANE), 1).astype(F32)
    onehot = jnp.where(lane == g_idx, 1.0, 0.0)
    before = _dot(tri_ref[...], onehot.astype(BF16))
    counts = jnp.sum(onehot, axis=0, keepdims=True)
    starts = jnp.zeros_like(counts)
    for k in range(1, N_GROUPS + 1):
        below = jnp.sum(jnp.where(lane[0:1] < k, counts, 0.0), axis=-1, keepdims=True)
        starts = starts + jnp.where(lane[0:1] == k, below, 0.0)
    pos = jnp.sum(onehot * (before + starts), axis=-1, keepdims=True)
    pos_b = jnp.broadcast_to(pos, (tm, LANE))
    pos_ref[...] = pos_b
    seg_ref[0] = jnp.broadcast_to(starts, (8, LANE)).astype(jnp.int32)
    pos_row = pos_b.T[0:1]
    row = lax.broadcasted_iota(jnp.int32, (tm, tm), 0).astype(F32)
    perm = jnp.where(row == pos_row, 1.0, 0.0).astype(BF16)
    hs_ref[...] = _dot(perm, h_hi).astype(BF16)
    cw_hi, cw_lo = _split_bf16(cw)
    cws_ref[...] = _dot(perm, cw_hi) + _dot(perm, cw_lo)


def _moe_expert_kernel(seg_ref, hs_ref, cws_ref, wg_ref, wu_ref, wd_ref, o_ref, acc_ref):
    tile, grp = pl.program_id(0), pl.program_id(1)
    tm = hs_ref.shape[0]

    @pl.when(grp == 0)
    def _():
        acc_ref[...] = jnp.zeros_like(acc_ref)

    seg_lo = seg_ref[tile * 8 + grp]
    seg_hi = seg_ref[tile * 8 + grp + 1]
    for r in range(tm // MOE_RB):
        rows = pl.ds(r * MOE_RB, MOE_RB)

        @pl.when((seg_lo < (r + 1) * MOE_RB) & (seg_hi > r * MOE_RB))
        def _():
            h = hs_ref[rows, :]
            hid = jax.nn.silu(_dot(h, wg_ref[0])) * _dot(h, wu_ref[0])
            cw = cws_ref[rows, :]
            lane = lax.broadcasted_iota(jnp.int32, cw.shape, 1)
            parts = []
            for e in range(EXP_PER_GROUP):
                col = jnp.sum(jnp.where(lane == N_GROUPS + grp * EXP_PER_GROUP + e, cw, 0.0), axis=-1, keepdims=True)
                parts.append((hid[:, e * D_FF_E:(e + 1) * D_FF_E] * col).astype(BF16))
            acc_ref[rows, :] += _dot(jnp.concatenate(parts, axis=-1), wd_ref[0])

    @pl.when(grp == N_GROUPS - 1)
    def _():
        o_ref[...] = acc_ref[...].astype(BF16)


def _moe_unsort_kernel(x_ref, ys_ref, pos_ref, fn_ref, o_ref, *, final_norm):
    tm = x_ref.shape[0]
    col = lax.broadcasted_iota(jnp.int32, (tm, tm), 1).astype(F32)
    unperm = jnp.where(col == pos_ref[:, 0:1], 1.0, 0.0).astype(BF16)
    y = x_ref[...] + _dot(unperm, ys_ref[...])
    if final_norm:
        y = _rms(y, fn_ref[...])
    o_ref[...] = y


def _moe(x2d, moe_norm, w_group, b_group, w_expert, b_expert, w_gate, w_up, w_down, fnorm, *, final_norm):
    t, d = x2d.shape
    tm = min(MOE_TM, t)
    nt = t // tm
    npad = LANE - N_GROUPS - N_EXPERTS
    wr = jnp.concatenate([w_group, w_expert, jnp.zeros((d, npad), F32)], axis=1)
    br = jnp.concatenate([b_group, b_expert, jnp.zeros((npad,), F32)])[None]
    wrh, wrl = _split_bf16(wr)
    ge = EXP_PER_GROUP * D_FF_E
    wg = w_gate.reshape(N_GROUPS, EXP_PER_GROUP, d, D_FF_E).transpose(0, 2, 1, 3).reshape(N_GROUPS, d, ge)
    wu = w_up.reshape(N_GROUPS, EXP_PER_GROUP, d, D_FF_E).transpose(0, 2, 1, 3).reshape(N_GROUPS, d, ge)
    wd = w_down.reshape(N_GROUPS, ge, d)
    tri = (np.arange(tm)[:, None] > np.arange(tm)[None, :]).astype(np.float32)
    full1 = lambda s: pl.BlockSpec(s, lambda i: tuple(0 for _ in s))
    row1 = lambda w: pl.BlockSpec((tm, w), lambda i: (i, 0))
    hs, cws, pos, seg = pl.pallas_call(
        _moe_route_kernel,
        grid=(nt,),
        in_specs=[row1(d), full1((1, d)), full1((d, LANE)), full1((d, LANE)), full1((1, LANE)), full1((tm, tm))],
        out_specs=[row1(d), row1(LANE), row1(LANE), pl.BlockSpec((1, 8, LANE), lambda i: (i, 0, 0))],
        out_shape=[jax.ShapeDtypeStruct((t, d), BF16), jax.ShapeDtypeStruct((t, LANE), F32),
                   jax.ShapeDtypeStruct((t, LANE), F32), jax.ShapeDtypeStruct((nt, 8, LANE), jnp.int32)],
        compiler_params=_cparams(("parallel",)),
        name="moe_route",
    )(x2d, moe_norm, wrh, wrl, br, jnp.asarray(tri, BF16))
    seg_flat = seg[:, 0, :8].reshape(nt * 8)
    ys = pl.pallas_call(
        _moe_expert_kernel,
        grid_spec=pltpu.PrefetchScalarGridSpec(
            num_scalar_prefetch=1,
            grid=(nt, N_GROUPS),
            in_specs=[pl.BlockSpec((tm, d), lambda i, j, s: (i, 0)),
                      pl.BlockSpec((tm, LANE), lambda i, j, s: (i, 0)),
                      pl.BlockSpec((1, d, ge), lambda i, j, s: (j, 0, 0)),
                      pl.BlockSpec((1, d, ge), lambda i, j, s: (j, 0, 0)),
                      pl.BlockSpec((1, ge, d), lambda i, j, s: (j, 0, 0))],
            out_specs=pl.BlockSpec((tm, d), lambda i, j, s: (i, 0)),
            scratch_shapes=[pltpu.VMEM((tm, d), F32)]),
        out_shape=jax.ShapeDtypeStruct((t, d), BF16),
        compiler_params=_cparams(("parallel", "arbitrary")),
        name="moe_experts",
    )(seg_flat, hs, cws, wg.astype(BF16), wu.astype(BF16), wd.astype(BF16))
    return pl.pallas_call(
        functools.partial(_moe_unsort_kernel, final_norm=final_norm),
        grid=(nt,),
        in_specs=[row1(d), row1(d), row1(LANE), full1((1, d))],
        out_specs=row1(d),
        out_shape=jax.ShapeDtypeStruct((t, d), F32),
        compiler_params=_cparams(("parallel",)),
        name="moe_unsort",
    )(x2d, ys, pos, fnorm)


def _inproj_weight(w_in):
    d = w_in.shape[0]
    kr = w_in[:, _IN_M + Q_LORA + KV_LORA:_IN_H]
    half = QK_ROPE // 2
    kr_rot = jnp.concatenate([-kr[:, half:], kr[:, :half]], axis=1)
    z64 = jnp.zeros((d, QK_NOPE), F32)
    z32 = jnp.zeros((d, LANE - QK_NOPE - QK_ROPE), F32)
    cols = [w_in[:, _IN_G:], w_in[:, _IN_A:_IN_S], w_in[:, _IN_H:_IN_G], w_in[:, _IN_S:_IN_M],
            w_in[:, _IN_M:_IN_M + Q_LORA], w_in[:, _IN_M + Q_LORA:_IN_M + Q_LORA + KV_LORA],
            z64, kr, z32, z64, kr_rot, z32]
    return jnp.concatenate(cols, axis=1).astype(BF16)


def _rope_tables(positions):
    inv_freq = 1.0 / (ROPE_BASE ** (jnp.arange(0, QK_ROPE, 2, dtype=F32) / QK_ROPE))
    ang = positions.astype(F32)[..., None] * inv_freq
    cos, sin = jnp.cos(ang), jnp.sin(ang)
    shp = positions.shape
    pad = jnp.zeros(shp + (HEAD_PAD - QK_NOPE - QK_ROPE,), F32)
    cos_t = jnp.concatenate([jnp.ones(shp + (QK_NOPE,), F32), cos, cos, pad], axis=-1)
    sin_t = jnp.concatenate([jnp.zeros(shp + (QK_NOPE,), F32), sin, sin, pad], axis=-1)
    return cos_t, sin_t


def kernel(x, mem, positions, mix_norm, w_in, gate_bias, conv_a, w_out_a, s5_lambda_re, s5_lambda_im, s5_log_step, s5_b_re, s5_b_im, s5_c_re, s5_c_im, s5_d, s5_w_glu, mla_q_norm, mla_w_q_b, mla_kv_norm, mla_w_kv_b, mla_w_o, hy_conv, hy_f_w1, hy_f_b1, hy_f_w2, hy_f_b2, hy_f_w3, hy_f_freq, hy_bias, hy_w_out, w_mix_out, xa_norm, mem_norm, xa_w_q, xa_w_kv, xa_w_o, moe_norm, moe_w_group, moe_b_group, moe_w_expert, moe_b_expert, moe_w_gate, moe_w_up, moe_w_down, final_norm):
    b, l, d = x.shape
    depth = w_in.shape[0]
    cos_t, sin_t = _rope_tables(positions)
    tables = _hy_tables(l)
    for i in range(depth):
        p3 = _inproj(x.reshape(b * l, d), mix_norm[i][None], _inproj_weight(w_in[i])).reshape(b, l, N_P)
        a_pre = _shortconv(p3, conv_a[i])
        ys = _s5(p3, _s5_operators(s5_lambda_re[i], s5_lambda_im[i], s5_log_step[i], s5_b_re[i], s5_b_im[i],
                                   s5_c_re[i], s5_c_im[i], s5_d[i]))
        wq, wkv = _mla_weights(mla_w_q_b[i], mla_w_kv_b[i])
        q, k, v = _mla_prep(p3, cos_t, sin_t, mla_q_norm[i][None], mla_kv_norm[i][None], wq, wkv)
        o_mla = _mla_attn(q, k, v)
        kf_re, kf_im = _hy_filter_spectrum(l, hy_f_w1[i], hy_f_b1[i], hy_f_w2[i], hy_f_b2[i], hy_f_w3[i],
                                           hy_f_freq[i], tables)
        z_hy = _hy_conv(p3, hy_conv[i], kf_re, kf_im, hy_bias[i], tables)
        x = _merge(x, p3, a_pre, ys, o_mla, z_hy, gate_bias[i], w_out_a[i], s5_w_glu[i], mla_w_o[i],
                   hy_w_out[i], w_mix_out[i])
        kv = _mem_kv(mem, mem_norm[None], xa_w_kv[i])
        x = _xattn(x, kv, xa_norm[i][None], xa_w_q[i], xa_w_o[i])
        x = _moe(x.reshape(b * l, d), moe_norm[i][None], moe_w_group[i], moe_b_group[i], moe_w_expert[i],
                 moe_b_expert[i], moe_w_gate[i], moe_w_up[i], moe_w_down[i], final_norm[None],
                 final_norm=(i == depth - 1)).reshape(b, l, d)
    return x
```

```python
import functools
import math

import numpy as np
import jax
import jax.numpy as jnp
from jax import lax
from jax.experimental import pallas as pl
from jax.experimental.pallas import tpu as pltpu

F32 = jnp.float32
BF16 = jnp.bfloat16
EPS = 1e-6

D_MODEL = 1024
N_BRANCH = 4
W_A = 512
W_S = 512
S5_GROUP = 16
S5_GROUPS = W_S // S5_GROUP
S5_STATE = 64
S5_CHUNK = 16
MLA_HEADS = 8
Q_LORA = 256
KV_LORA = 256
QK_NOPE = 64
QK_ROPE = 32
V_DIM = 64
ROPE_BASE = 10000.0
W_H = 512
HY_ORDER = 2
HY_EMB = 33
HY_BANDS = (HY_EMB - 1) // 2
HY_FO = 64
HY_FAST_DECAY = 0.3
HY_SLOW_DECAY = 1.5
HY_TARGET = 1e-2
XA_HEADS = 4
XA_DH = 128
N_GROUPS = 4
EXP_PER_GROUP = 8
N_EXPERTS = N_GROUPS * EXP_PER_GROUP
D_FF_E = 256

LANE = 128
HEAD_PAD = 128
VMEM_LIMIT = 56 * 1024 * 1024

OFF_G = 0
OFF_A = OFF_G + N_BRANCH * D_MODEL
OFF_H = OFF_A + 3 * W_A
OFF_S = OFF_H + 3 * W_H
OFF_CQ = OFF_S + W_S
OFF_CKV = OFF_CQ + Q_LORA
OFF_KRA = OFF_CKV + KV_LORA
OFF_KRB = OFF_KRA + LANE
N_P = OFF_KRB + LANE

_IN_A = 0
_IN_S = 3 * W_A
_IN_M = _IN_S + W_S
_IN_H = _IN_M + Q_LORA + KV_LORA + QK_ROPE
_IN_G = _IN_H + 3 * W_H

HY_N1 = 8
HY_CT = 128
HY_NCT = W_H // HY_CT


def _cparams(sem, vmem=VMEM_LIMIT):
    return pltpu.CompilerParams(dimension_semantics=sem, vmem_limit_bytes=vmem)


def _split_bf16(x):
    hi = x.astype(BF16)
    lo = (x - hi.astype(F32)).astype(BF16)
    return hi, lo


def _dot(a, b):
    return jnp.dot(a, b, preferred_element_type=F32)


def _dot3(a_hi, a_lo, b_hi, b_lo):
    return _dot(a_hi, b_hi) + (_dot(a_lo, b_hi) + _dot(a_hi, b_lo))


def _rms(x, g):
    return x * lax.rsqrt(jnp.mean(x * x, axis=-1, keepdims=True) + EPS) * g


def _inproj_kernel(x_ref, g_ref, w_ref, o_ref, h_ref):
    @pl.when(pl.program_id(1) == 0)
    def _():
        h_ref[...] = _rms(x_ref[...], g_ref[...]).astype(BF16)

    o_ref[...] = _dot(h_ref[...], w_ref[...]).astype(o_ref.dtype)


def _inproj(x2d, g, w, *, tm=1024, tn=N_P // 3):
    t = x2d.shape[0]
    return pl.pallas_call(
        _inproj_kernel,
        grid=(t // tm, N_P // tn),
        in_specs=[pl.BlockSpec((tm, D_MODEL), lambda i, j: (i, 0)),
                  pl.BlockSpec((1, D_MODEL), lambda i, j: (0, 0)),
                  pl.BlockSpec((D_MODEL, tn), lambda i, j: (0, j))],
        out_specs=pl.BlockSpec((tm, tn), lambda i, j: (i, j)),
        out_shape=jax.ShapeDtypeStruct((t, N_P), BF16),
        scratch_shapes=[pltpu.VMEM((tm, D_MODEL), BF16)],
        compiler_params=_cparams(("parallel", "arbitrary")),
        name="inproj",
    )(x2d, g, w)


def _conv3(u, w):
    n = u.shape[0]
    row = lax.broadcasted_iota(jnp.int32, u.shape, 0)
    prev = jnp.where(row == 0, 0.0, pltpu.roll(u, 1, axis=0))
    nxt = jnp.where(row == n - 1, 0.0, pltpu.roll(u, n - 1, axis=0))
    return w[0:1] * prev + w[1:2] * u + w[2:3] * nxt


def _shortconv_kernel(bg_ref, cg_ref, xi_ref, wa_ref, a_ref):
    f32 = lambda r: r[0].astype(F32)
    a_ref[0] = f32(bg_ref) * _conv3(f32(cg_ref) * f32(xi_ref), wa_ref[0])


def _shortconv(p3, conv_a):
    b, l, _ = p3.shape
    nct = W_A // LANE
    wa = conv_a.reshape(3, nct, LANE).transpose(1, 0, 2)

    def pspec(off):
        return pl.BlockSpec((1, l, LANE), lambda i, j, off=off: (i, 0, off // LANE + j))

    return pl.pallas_call(
        _shortconv_kernel,
        grid=(b, nct),
        in_specs=[pspec(OFF_A), pspec(OFF_A + W_A), pspec(OFF_A + 2 * W_A),
                  pl.BlockSpec((1, 3, LANE), lambda i, j: (j, 0, 0))],
        out_specs=pl.BlockSpec((1, l, LANE), lambda i, j: (i, 0, j)),
        out_shape=jax.ShapeDtypeStruct((b, l, W_A), F32),
        compiler_params=_cparams(("parallel", "parallel")),
        name="shortconv",
    )(p3, p3, p3, wa)


S5_GPB = LANE // S5_GROUP
S5_NLB = W_S // LANE
S5_XW = S5_CHUNK * LANE
S5_SW = S5_GPB * 2 * S5_STATE


def _s5_operators(lam_re, lam_im, log_step, b_re, b_im, c_re, c_im, d_skip):
    q, hh, g = S5_CHUNK, S5_GROUP, S5_GROUPS
    hp = lax.Precision.HIGHEST
    delta = jnp.exp(log_step)[..., None]

    def powers(exps):
        e = jnp.asarray(exps, F32)
        mag = jnp.exp((lam_re * delta)[..., None] * e)
        ang = (lam_im * delta)[..., None] * e
        return mag * jnp.cos(ang), mag * jnp.sin(ang)

    ramp = np.arange(q)
    p1r, p1i = powers([1.0])
    den = lam_re * lam_re + lam_im * lam_im
    nr, ni = p1r[..., 0] - 1.0, p1i[..., 0]
    fr = (nr * lam_re + ni * lam_im) / den
    fi = (ni * lam_re - nr * lam_im) / den
    bbr = fr[..., None] * b_re - fi[..., None] * b_im
    bbi = fr[..., None] * b_im + fi[..., None] * b_re

    def times_bbar(exps):
        p_r, p_i = powers(exps)
        return (p_r[..., None] * bbr[..., None, :] - p_i[..., None] * bbi[..., None, :],
                p_r[..., None] * bbi[..., None, :] + p_i[..., None] * bbr[..., None, :])

    mr, mi = times_bbar(ramp)
    kern = (jnp.einsum('dghp,dgpek->dgehk', c_re, mr, precision=hp)
            - jnp.einsum('dghp,dgpek->dgehk', c_im, mi, precision=hp))
    i_idx = jnp.arange(q)[:, None]
    j_idx = jnp.arange(q)[None, :]
    diff_f = jnp.clip(j_idx - i_idx, 0, q - 1)
    diff_b = jnp.clip(i_idx - j_idx, 0, q - 1)
    t_f = jnp.where((i_idx <= j_idx)[None, :, :, None, None], kern[0][:, diff_f], 0.0)
    t_b = jnp.where((i_idx >= j_idx)[None, :, :, None, None], kern[1][:, diff_b], 0.0)
    t_all = (t_f + t_b).transpose(0, 1, 4, 2, 3)
    skip = jnp.eye(q, dtype=F32)[:, None, :, None] * jnp.eye(hh, dtype=F32)[None, :, None, :]
    t_all = t_all + skip[None] * d_skip.reshape(g, 1, 1, 1, hh)
    eye_g = jnp.eye(S5_GPB, dtype=BF16)
    blk = lambda a: a.astype(BF16).reshape((S5_NLB, S5_GPB) + a.shape[1:])
    r_idx = lax.broadcasted_iota(jnp.int32, (S5_GPB, q * hh, S5_XW), 1)
    n_idx = lax.broadcasted_iota(jnp.int32, (S5_GPB, q * hh, S5_XW), 2)
    g_idx = lax.broadcasted_iota(jnp.int32, (S5_GPB, q * hh, S5_XW), 0)
    spread = (n_idx == (r_idx // hh) * LANE + g_idx * hh + r_idx % hh).astype(BF16)
    cols_big = lambda a: jnp.einsum('lgrc,gcn->lgrn', blk(a), spread, preferred_element_type=BF16)
    t_big = cols_big(t_all.reshape(g, q * hh, q * hh)).reshape(S5_NLB, S5_GPB, q, hh, S5_XW)
    t_big = t_big.transpose(0, 2, 1, 3, 4).reshape(S5_NLB, S5_XW, S5_XW)

    er, ei = times_bbar(q - 1 - ramp)
    state_in = lambda f, b: jnp.concatenate([f[0].transpose(0, 2, 3, 1), b[1].transpose(0, 2, 3, 1)], axis=-1)

    def st_big(a):
        a5 = blk(a).transpose(0, 2, 1, 3, 4)
        return (a5[:, :, :, :, None, :] * eye_g[None, None, :, None, :, None]).reshape(S5_NLB, S5_XW, S5_SW)

    w1 = jnp.concatenate([t_big, st_big(state_in(er, mr)), st_big(state_in(ei, mi))], axis=-1)

    def coef(c_r, c_i, p_r, p_i):
        cr, ci = c_r.transpose(0, 2, 1)[:, :, None, :], c_i.transpose(0, 2, 1)[:, :, None, :]
        return cr * p_r[..., None] - ci * p_i[..., None], -(cr * p_i[..., None] + ci * p_r[..., None])

    pf_r, pf_i = powers(ramp + 1)
    pb_r, pb_i = powers(q - ramp)
    f_re, f_im = coef(c_re[0], c_im[0], pf_r[0], pf_i[0])
    r_re, r_im = coef(c_re[1], c_im[1], pb_r[1], pb_i[1])
    out_big = lambda a: cols_big(a.reshape(g, 2 * S5_STATE, q * hh)).reshape(S5_NLB, S5_SW, S5_XW)
    wout = jnp.concatenate([out_big(jnp.concatenate([f_re, r_re], axis=1)),
                            out_big(jnp.concatenate([f_im, r_im], axis=1))], axis=1)

    lanes = lambda a: jnp.concatenate([a[0], a[1]], axis=-1).reshape(S5_NLB, S5_SW)
    pq_r, pq_i = powers([float(q)])
    lam = jnp.stack([lanes(pq_r[..., 0]), lanes(pq_i[..., 0])], axis=1)
    return w1, wout, lam


def _s5_kernel(u_ref, w1_ref, wo_ref, lam_ref, y_ref, u32_ref, yin_ref, sre_ref, sim_ref, xre_ref, xim_ref,
               *, n_chunks):
    q, gpb, half = S5_CHUNK, S5_GPB, S5_STATE
    u32_ref[...] = u_ref[0].astype(F32)
    xcat = jnp.concatenate([u32_ref[pl.ds(i, n_chunks, stride=q), :].astype(BF16) for i in range(q)], axis=-1)
    m1 = _dot(xcat, w1_ref[0])
    yin_ref[...] = m1[:, :S5_XW]
    for g in range(gpb):
        sre_ref[pl.ds(g, n_chunks, stride=gpb), :] = m1[:, S5_XW + g * LANE:S5_XW + (g + 1) * LANE]
        sim_ref[pl.ds(g, n_chunks, stride=gpb), :] = m1[:, S5_XW + S5_SW + g * LANE:S5_XW + S5_SW + (g + 1) * LANE]
    lr = lam_ref[0, 0]
    li = lam_ref[0, 1]
    fwd_lane = lax.broadcasted_iota(jnp.int32, (gpb, LANE), 1) < half

    def step(k, carry):
        xr, xi = carry
        rf = pl.multiple_of(k * gpb, gpb)
        rb = pl.multiple_of((n_chunks - 1 - k) * gpb, gpb)
        xre_ref[pl.ds(rf, gpb), 0:half] = xr[:, 0:half]
        xim_ref[pl.ds(rf, gpb), 0:half] = xi[:, 0:half]
        xre_ref[pl.ds(rb, gpb), half:LANE] = xr[:, half:LANE]
        xim_ref[pl.ds(rb, gpb), half:LANE] = xi[:, half:LANE]
        ar = jnp.where(fwd_lane, sre_ref[pl.ds(rf, gpb), :], sre_ref[pl.ds(rb, gpb), :])
        ai = jnp.where(fwd_lane, sim_ref[pl.ds(rf, gpb), :], sim_ref[pl.ds(rb, gpb), :])
        return lr * xr - li * xi + ar, lr * xi + li * xr + ai

    zero = jnp.zeros((gpb, LANE), F32)
    lax.fori_loop(0, n_chunks, step, (zero, zero))
    xs = jnp.concatenate([r[pl.ds(g, n_chunks, stride=gpb), :].astype(BF16)
                          for r in (xre_ref, xim_ref) for g in range(gpb)], axis=-1)
    y = yin_ref[...] + _dot(xs, wo_ref[0])
    for j in range(q):
        y_ref[0, pl.ds(j, n_chunks, stride=q), :] = y[:, j * LANE:(j + 1) * LANE]


def _s5(p3, ops):
    b, l, _ = p3.shape
    nc = l // S5_CHUNK
    w1, wout, lam = ops
    lam = lam.reshape(S5_NLB, 2, S5_GPB, LANE)
    once = dict(pipeline_mode=pl.Buffered(1))
    return pl.pallas_call(
        functools.partial(_s5_kernel, n_chunks=nc),
        grid=(S5_NLB, b),
        in_specs=[pl.BlockSpec((1, l, LANE), lambda j, i: (i, 0, OFF_S // LANE + j)),
                  pl.BlockSpec((1, S5_XW, S5_XW + 2 * S5_SW), lambda j, i: (j, 0, 0), **once),
                  pl.BlockSpec((1, 2 * S5_SW, S5_XW), lambda j, i: (j, 0, 0), **once),
                  pl.BlockSpec((1, 2, S5_GPB, LANE), lambda j, i: (j, 0, 0, 0))],
        out_specs=pl.BlockSpec((1, l, LANE), lambda j, i: (i, 0, j)),
        out_shape=jax.ShapeDtypeStruct((b, l, W_S), F32),
        scratch_shapes=[pltpu.VMEM((l, LANE), F32), pltpu.VMEM((nc, S5_XW), F32)]
        + [pltpu.VMEM((nc * S5_GPB, LANE), F32)] * 4,
        compiler_params=_cparams(("parallel", "arbitrary")),
        name="s5_scan",
    )(p3, w1, wout, lam)


def _mla_prep_kernel(cq_ref, ckv_ref, kra_ref, krb_ref, cos_ref, sin_ref, qn_ref, kvn_ref,
                     wq_ref, wkv_ref, q_ref, k_ref, v_ref):
    cqn = _rms(cq_ref[0].astype(F32), qn_ref[...]).astype(BF16)
    ckvn = _rms(ckv_ref[0].astype(F32), kvn_ref[...]).astype(BF16)
    cos = cos_ref[0]
    sin = sin_ref[0]
    kr = kra_ref[0].astype(F32) * cos + krb_ref[0].astype(F32) * sin
    lane = lax.broadcasted_iota(jnp.int32, cos.shape, 1)
    ones_col = jnp.where(lane == V_DIM, 1.0, 0.0)
    scale = (QK_NOPE + QK_ROPE) ** -0.5
    for h in range(MLA_HEADS):
        qq = _dot(cqn, wq_ref[h])
        q_ref[0, h] = ((qq[:, :HEAD_PAD] * cos + qq[:, HEAD_PAD:] * sin) * scale).astype(BF16)
        kv = _dot(ckvn, wkv_ref[h])
        k_ref[0, h] = (kv[:, :HEAD_PAD] + kr).astype(BF16)
        v_ref[0, h] = (kv[:, HEAD_PAD:] + ones_col).astype(BF16)


def _mla_weights(w_q_b, w_kv_b):
    dq = QK_NOPE + QK_ROPE
    half = QK_ROPE // 2
    wq = w_q_b.reshape(Q_LORA, MLA_HEADS, dq).transpose(1, 0, 2)
    rope = wq[..., QK_NOPE:]
    rot = jnp.concatenate([-rope[..., half:], rope[..., :half]], axis=-1)
    zq = jnp.zeros((MLA_HEADS, Q_LORA, HEAD_PAD - dq), F32)
    zn = jnp.zeros((MLA_HEADS, Q_LORA, QK_NOPE), F32)
    wq_full = jnp.concatenate([wq, zq, zn, rot, zq], axis=-1)
    wkv = w_kv_b.reshape(KV_LORA, MLA_HEADS, QK_NOPE + V_DIM).transpose(1, 0, 2)
    zk = jnp.zeros((MLA_HEADS, KV_LORA, HEAD_PAD - QK_NOPE), F32)
    zv = jnp.zeros((MLA_HEADS, KV_LORA, HEAD_PAD - V_DIM), F32)
    wkv_full = jnp.concatenate([wkv[..., :QK_NOPE], zk, wkv[..., QK_NOPE:], zv], axis=-1)
    return wq_full.astype(BF16), wkv_full.astype(BF16)


def _mla_prep(p3, cos_t, sin_t, q_norm, kv_norm, wq, wkv, *, tl=512):
    b, l, _ = p3.shape
    hspec = pl.BlockSpec((1, MLA_HEADS, tl, HEAD_PAD), lambda i, j: (i, 0, j, 0))
    hshape = jax.ShapeDtypeStruct((b, MLA_HEADS, l, HEAD_PAD), BF16)
    tab = pl.BlockSpec((1, tl, LANE), lambda i, j: (i, j, 0))
    return pl.pallas_call(
        _mla_prep_kernel,
        grid=(b, l // tl),
        in_specs=[pl.BlockSpec((1, tl, Q_LORA), lambda i, j: (i, j, OFF_CQ // Q_LORA)),
                  pl.BlockSpec((1, tl, KV_LORA), lambda i, j: (i, j, OFF_CKV // KV_LORA)),
                  pl.BlockSpec((1, tl, LANE), lambda i, j: (i, j, OFF_KRA // LANE)),
                  pl.BlockSpec((1, tl, LANE), lambda i, j: (i, j, OFF_KRB // LANE)),
                  tab, tab,
                  pl.BlockSpec((1, Q_LORA), lambda i, j: (0, 0)),
                  pl.BlockSpec((1, KV_LORA), lambda i, j: (0, 0)),
                  pl.BlockSpec((MLA_HEADS, Q_LORA, 2 * HEAD_PAD), lambda i, j: (0, 0, 0)),
                  pl.BlockSpec((MLA_HEADS, KV_LORA, 2 * HEAD_PAD), lambda i, j: (0, 0, 0))],
        out_specs=[hspec, hspec, hspec],
        out_shape=[hshape, hshape, hshape],
        compiler_params=_cparams(("parallel", "parallel")),
        name="mla_prep",
    )(p3, p3, p3, p3, cos_t, sin_t, q_norm, kv_norm, wq, wkv)


def _mla_attn_kernel(q_ref, k_ref, v_ref, o_ref, *, tq):
    def q_tile(t, carry):
        rows = pl.ds(pl.multiple_of(t * tq, tq), tq)
        outs = []
        for h in range(2):
            s = lax.dot_general(q_ref[0, h, rows, :], k_ref[0, h], (((1,), (1,)), ((), ())),
                                preferred_element_type=F32)
            m = jnp.max(s, axis=-1, keepdims=True)
            p = jnp.exp(s - m).astype(BF16)
            o = _dot(p, v_ref[0, h])
            outs.append(o / o[:, V_DIM:V_DIM + 1])
        lane = lax.broadcasted_iota(jnp.int32, outs[0].shape, 1)
        o_ref[0, rows, :] = jnp.where(lane < V_DIM, outs[0], pltpu.roll(outs[1], V_DIM, axis=1))
        return carry

    lax.fori_loop(0, q_ref.shape[2] // tq, q_tile, 0)


def _mla_attn(q, k, v, *, tq=256):
    b, _, l, _ = q.shape
    hspec = pl.BlockSpec((1, 2, l, HEAD_PAD), lambda i, j: (i, j, 0, 0))
    return pl.pallas_call(
        functools.partial(_mla_attn_kernel, tq=tq),
        grid=(b, MLA_HEADS // 2),
        in_specs=[hspec, hspec, hspec],
        out_specs=pl.BlockSpec((1, l, 2 * V_DIM), lambda i, j: (i, 0, j)),
        out_shape=jax.ShapeDtypeStruct((b, l, MLA_HEADS * V_DIM), F32),
        compiler_params=_cparams(("parallel", "parallel")),
        name="mla_attn",
    )(q, k, v)


def _hy_sizes(l):
    n = 2 * l
    n2 = n // HY_N1
    nf = n2 // 2 + 1
    nfp = ((nf + 63) // 64) * 64
    return n, n2, nf, nfp


def _hy_tables(l):
    n, n2, nf, nfp = _hy_sizes(l)
    f2 = np.arange(nfp)[:, None].astype(np.float64)
    t2 = np.arange(n2 // 2)[None, :].astype(np.float64)
    valid = (np.arange(nfp) < nf)[:, None]
    ang = 2.0 * np.pi * f2 * t2 / n2
    fwd = np.concatenate([np.where(valid, np.cos(ang), 0.0), np.where(valid, -np.sin(ang), 0.0)], axis=0)
    wgt = np.where((np.arange(nfp) == 0) | (np.arange(nfp) == nf - 1), 1.0, 2.0)[:, None] * valid / n
    inv = np.concatenate([(wgt * np.cos(ang)).T, (-wgt * np.sin(ang)).T], axis=1)
    t1 = np.arange(HY_N1)[None, :].astype(np.float64)
    tw_ang = 2.0 * np.pi * f2 * t1 / n
    tw_re = np.repeat(np.cos(tw_ang), HY_CT, axis=1).astype(np.float32)
    tw_im = np.repeat(-np.sin(tw_ang), HY_CT, axis=1).astype(np.float32)
    fwd_hi, fwd_lo = _split_bf16(jnp.asarray(fwd, F32))
    return fwd_hi, fwd_lo, jnp.asarray(inv, F32).astype(BF16), jnp.asarray(tw_re), jnp.asarray(tw_im)


def _cmul(a, b):
    return a[0] * b[0] - a[1] * b[1], a[0] * b[1] + a[1] * b[0]


def _cadd(a, b):
    return a[0] + b[0], a[1] + b[1]


def _csub(a, b):
    return a[0] - b[0], a[1] - b[1]


def _cmul_i(a, sign):
    return (-a[1], a[0]) if sign > 0 else (a[1], -a[0])


def _fft4(a, sign):
    s0, s1 = _cadd(a[0], a[2]), _csub(a[0], a[2])
    s2, s3 = _cadd(a[1], a[3]), _csub(a[1], a[3])
    r3 = _cmul_i(s3, sign)
    return [_cadd(s0, s2), _cadd(s1, r3), _csub(s0, s2), _csub(s1, r3)]


def _fft8(x, sign):
    e = _fft4([x[0], x[2], x[4], x[6]], sign)
    o = _fft4([x[1], x[3], x[5], x[7]], sign)
    r = math.sqrt(0.5)
    o1 = ((o[1][0] - sign * o[1][1]) * r, (o[1][1] + sign * o[1][0]) * r)
    o2 = _cmul_i(o[2], sign)
    o3 = ((-o[3][0] - sign * o[3][1]) * r, (-o[3][1] + sign * o[3][0]) * r)
    tw = [o[0], o1, o2, o3]
    return [_cadd(e[k], tw[k]) for k in range(4)] + [_csub(e[k], tw[k]) for k in range(4)]


def _blocks(ref_re, ref_im, rows):
    return [(ref_re[rows, k * HY_CT:(k + 1) * HY_CT], ref_im[rows, k * HY_CT:(k + 1) * HY_CT])
            for k in range(HY_N1)]


def _hy_spectrum(z_ref, twr_ref, twi_ref, rows, nfp):
    t = []
    for k in range(HY_N1):
        sl = slice(k * HY_CT, (k + 1) * HY_CT)
        zk = (z_ref[rows, sl], z_ref[pl.ds(nfp + rows.start, rows.size), sl])
        t.append(_cmul(zk, (twr_ref[rows, sl], twi_ref[rows, sl])))
    return _fft8(t, -1)


def _hy_fold(nat_ref, rows):
    return jnp.concatenate([nat_ref[pl.ds(t1, rows, stride=HY_N1), :] for t1 in range(HY_N1)], axis=-1)


def _hy_filter_kernel(z_ref, w1_ref, b1_ref, w2_ref, b2_ref, w3_ref, fr_ref, tn_ref, dl_ref,
                      fh_ref, fl_ref, twr_ref, twi_ref, kr_ref, ki_ref, ff_ref, fb_ref, zf_ref, zb_ref,
                      *, nfp, row_chunk, rows):
    hp = lax.Precision.HIGHEST
    fr = fr_ref[...]
    h = jnp.sin(fr * (jnp.dot(z_ref[...], w1_ref[...], precision=hp, preferred_element_type=F32) + b1_ref[...]))
    h = jnp.sin(fr * (jnp.dot(h, w2_ref[...], precision=hp, preferred_element_type=F32) + b2_ref[...]))
    filt = jnp.dot(h, w3_ref[0, 0], precision=hp, preferred_element_type=F32)
    decay = jnp.exp(-tn_ref[...] * dl_ref[0])
    fwd = filt[:, :HY_CT] * decay
    row = lax.broadcasted_iota(jnp.int32, fwd.shape, 0)
    bwd = jnp.where(row == 0, 0.0, filt[:, HY_CT:] * decay)
    inv = lax.rsqrt(jnp.sum(fwd * fwd, axis=0, keepdims=True) + jnp.sum(bwd * bwd, axis=0, keepdims=True) + EPS)
    ff_ref[...] = fwd
    fb_ref[...] = bwd
    for src, dst in ((ff_ref, zf_ref), (fb_ref, zb_ref)):
        k_hi, k_lo = _split_bf16(_hy_fold(src, rows))
        dst[...] = _dot3(fh_ref[...], fl_ref[...], k_hi, k_lo)
    for c in range(nfp // row_chunk):
        rws = pl.ds(c * row_chunk, row_chunk)
        sf = _hy_spectrum(zf_ref, twr_ref, twi_ref, rws, nfp)
        sb = _hy_spectrum(zb_ref, twr_ref, twi_ref, rws, nfp)
        for f1 in range(HY_N1):
            sl = slice(f1 * HY_CT, (f1 + 1) * HY_CT)
            kr_ref[0, 0, rws, sl] = (sf[f1][0] + sb[f1][0]) * inv
            ki_ref[0, 0, rws, sl] = (sf[f1][1] - sb[f1][1]) * inv


def _hy_row_chunk(nfp):
    for c in (96, 72, 64, 48, 32, 16, 8):
        if nfp % c == 0:
            return c
    return nfp


def _hy_filter_spectrum(l, w1, b1, w2, b2, w3, freq, tables):
    n, n2, nf, nfp = _hy_sizes(l)
    fwd_hi, fwd_lo, _, tw_re, tw_im = tables
    rows = l // HY_N1
    wide = HY_N1 * HY_CT
    t = np.arange(l, dtype=np.float32)
    t_norm = t / np.float32(max(l - 1, 1))
    bands = np.linspace(1e-4, HY_BANDS - 1, HY_BANDS, dtype=np.float32)
    ang = np.float32(2.0 * math.pi / l) * t[:, None] * bands[None]
    z = np.concatenate([t_norm[:, None], np.cos(ang), -np.sin(ang)], axis=-1).astype(np.float32)
    kpad = 40
    z = np.pad(z, ((0, 0), (0, kpad - HY_EMB)))
    w1p = jnp.pad(w1, ((0, kpad - HY_EMB), (0, 0)))
    max_decay = math.log(HY_TARGET) / HY_FAST_DECAY
    min_decay = math.log(HY_TARGET) / HY_SLOW_DECAY
    deltas = np.abs(np.linspace(min_decay, max_decay, W_H, dtype=np.float32)).reshape(HY_NCT, 1, HY_CT)
    w3t = w3.reshape(HY_FO, HY_ORDER, 2, HY_NCT, HY_CT).transpose(1, 3, 0, 2, 4).reshape(HY_ORDER, HY_NCT, HY_FO, 2 * HY_CT)
    full = lambda s: pl.BlockSpec(s, lambda i, j: (0, 0))
    ospec = pl.BlockSpec((1, 1, nfp, wide), lambda i, j: (i, j, 0, 0))
    oshape = jax.ShapeDtypeStruct((HY_ORDER, HY_NCT, nfp, wide), F32)
    return pl.pallas_call(
        functools.partial(_hy_filter_kernel, nfp=nfp, row_chunk=_hy_row_chunk(nfp), rows=rows),
        grid=(HY_ORDER, HY_NCT),
        in_specs=[full((l, kpad)), full((kpad, HY_FO)), full((1, HY_FO)), full((HY_FO, HY_FO)), full((1, HY_FO)),
                  pl.BlockSpec((1, 1, HY_FO, 2 * HY_CT), lambda i, j: (i, j, 0, 0)),
                  full((1, HY_FO)), full((l, 1)),
                  pl.BlockSpec((1, 1, HY_CT), lambda i, j: (j, 0, 0)),
                  full((2 * nfp, rows)), full((2 * nfp, rows)), full((nfp, wide)), full((nfp, wide))],
        out_specs=[ospec, ospec],
        out_shape=[oshape, oshape],
        scratch_shapes=[pltpu.VMEM((l, HY_CT), F32), pltpu.VMEM((l, HY_CT), F32),
                        pltpu.VMEM((2 * nfp, wide), F32), pltpu.VMEM((2 * nfp, wide), F32)],
        compiler_params=_cparams(("parallel", "parallel")),
        name="hyena_filter",
    )(jnp.asarray(z), w1p, b1[None], w2, b2[None], w3t, freq[None], jnp.asarray(t_norm[:, None]),
      jnp.asarray(deltas), fwd_hi, fwd_lo, tw_re, tw_im)


def _hy_conv_kernel(pv_ref, p1_ref, p2_ref, wc_ref, kr_ref, ki_ref, bias_ref, fwd_ref, inv_ref, twr_ref, twi_ref,
                    o_ref, nat_ref, z_ref, u_ref, *, nfp, row_chunk, rows):
    def folded_conv3(p_ref, part):
        nat_ref[...] = _conv3(p_ref[0].astype(F32), wc_ref[0, part])
        return _hy_fold(nat_ref, rows)

    def long_conv(u, order):
        z_ref[...] = _dot(fwd_ref[...], u.astype(BF16))
        for c in range(nfp // row_chunk):
            rows = pl.ds(c * row_chunk, row_chunk)
            spec = _hy_spectrum(z_ref, twr_ref, twi_ref, rows, nfp)
            kf = _blocks(kr_ref.at[order, 0], ki_ref.at[order, 0], rows)
            y = _fft8([_cmul(spec[f1], kf[f1]) for f1 in range(HY_N1)], +1)
            for t1 in range(HY_N1):
                sl = slice(t1 * HY_CT, (t1 + 1) * HY_CT)
                w = _cmul(y[t1], (twr_ref[rows, sl], -twi_ref[rows, sl]))
                u_ref[rows, sl] = w[0].astype(BF16)
                u_ref[pl.ds(nfp + c * row_chunk, row_chunk), sl] = w[1].astype(BF16)
        return _dot(inv_ref[...], u_ref[...])

    v = folded_conv3(pv_ref, 0)
    x1 = folded_conv3(p1_ref, 1)
    x2 = folded_conv3(p2_ref, 2)
    bias = bias_ref[0]
    z1 = x1 * (long_conv(v, 0) + v * bias[0:1])
    z2 = x2 * (long_conv(z1, 1) + z1 * bias[1:2])
    for t1 in range(HY_N1):
        o_ref[0, 0, pl.ds(t1, rows, stride=HY_N1), :] = z2[:, t1 * HY_CT:(t1 + 1) * HY_CT]


def _hy_conv(p3, hy_conv, kf_re, kf_im, bias, tables):
    b, l, _ = p3.shape
    n, n2, nf, nfp = _hy_sizes(l)
    fwd_hi, _, inv_t, tw_re, tw_im = tables
    rows = l // HY_N1
    wide = HY_N1 * HY_CT
    wc = hy_conv.reshape(3, 3, HY_NCT, HY_CT).transpose(2, 1, 0, 3)
    bias_t = jnp.tile(bias.reshape(HY_ORDER, HY_NCT, 1, HY_CT), (1, 1, HY_N1, 1))
    bias_t = bias_t.transpose(1, 0, 2, 3).reshape(HY_NCT, HY_ORDER, wide)
    once = dict(pipeline_mode=pl.Buffered(1))

    def pspec(part):
        return pl.BlockSpec((1, l, HY_CT), lambda j, i, part=part: (i, 0, (OFF_H + part * W_H) // HY_CT + j))

    kspec = pl.BlockSpec((HY_ORDER, 1, nfp, wide), lambda j, i: (0, j, 0, 0), **once)
    full = lambda s: pl.BlockSpec(s, lambda j, i: (0, 0), **once)
    return pl.pallas_call(
        functools.partial(_hy_conv_kernel, nfp=nfp, row_chunk=_hy_row_chunk(nfp), rows=rows),
        grid=(HY_NCT, b),
        in_specs=[pspec(0), pspec(1), pspec(2),
                  pl.BlockSpec((1, 3, 3, HY_CT), lambda j, i: (j, 0, 0, 0)),
                  kspec, kspec,
                  pl.BlockSpec((1, HY_ORDER, wide), lambda j, i: (j, 0, 0)),
                  full((2 * nfp, rows)), full((rows, 2 * nfp)), full((nfp, wide)), full((nfp, wide))],
        out_specs=pl.BlockSpec((1, 1, l, HY_CT), lambda j, i: (i, j, 0, 0)),
        out_shape=jax.ShapeDtypeStruct((b, HY_NCT, l, HY_CT), F32),
        scratch_shapes=[pltpu.VMEM((l, HY_CT), F32), pltpu.VMEM((2 * nfp, wide), F32),
                        pltpu.VMEM((2 * nfp, wide), BF16)],
        compiler_params=_cparams(("parallel", "arbitrary")),
        name="hyena_conv",
    )(p3, p3, p3, wc, kf_re, kf_im, bias_t, fwd_hi, inv_t, tw_re, tw_im)


def _merge_kernel(x_ref, pg_ref, a_ref, ys_ref, om_ref, zh_ref, gb_ref, wa_ref, wglu_ref, wo_ref,
                  wh_ref, wmix_ref, o_ref):
    d = D_MODEL
    y_a = _dot(a_ref[0].astype(BF16), wa_ref[...])
    glu = _dot(jax.nn.gelu(ys_ref[0]).astype(BF16), wglu_ref[...])
    y_s = glu[:, :d] * jax.nn.sigmoid(glu[:, d:])
    y_m = _dot(om_ref[0].astype(BF16), wo_ref[...])
    y_h = _dot(zh_ref[0, 0].astype(BF16), wh_ref[0])
    for c in range(1, HY_NCT):
        y_h = y_h + _dot(zh_ref[0, c].astype(BF16), wh_ref[c])
    gb = gb_ref[...]
    merged = jnp.zeros_like(y_a)
    for i, y in enumerate((y_a, y_s, y_m, y_h)):
        merged = merged + jax.nn.sigmoid(pg_ref[0, :, i * d:(i + 1) * d].astype(F32) + gb[i:i + 1]) * y
    o_ref[0] = x_ref[0] + _dot(merged.astype(BF16), wmix_ref[...])


def _merge(x, p3, a_pre, ys, o_mla, z_hy, gate_bias, w_out_a, w_glu, w_o, hy_w_out, w_mix, *, tm=512):
    b, l, d = x.shape
    row = lambda w: pl.BlockSpec((1, tm, w), lambda i, j: (i, j, 0))
    full = lambda s: pl.BlockSpec(s, lambda i, j: tuple(0 for _ in s))
    return pl.pallas_call(
        _merge_kernel,
        grid=(b, l // tm),
        in_specs=[row(d), row(N_BRANCH * d), row(W_A), row(W_S), row(MLA_HEADS * V_DIM),
                  pl.BlockSpec((1, HY_NCT, tm, HY_CT), lambda i, j: (i, 0, j, 0)),
                  full((N_BRANCH, d)), full((W_A, d)), full((W_S, 2 * d)), full((MLA_HEADS * V_DIM, d)),
                  full((HY_NCT, HY_CT, d)), full((d, d))],
        out_specs=row(d),
        out_shape=jax.ShapeDtypeStruct((b, l, d), F32),
        compiler_params=_cparams(("parallel", "parallel")),
        name="merge",
    )(x, p3, a_pre, ys, o_mla, z_hy, gate_bias, w_out_a.astype(BF16), w_glu.astype(BF16),
      w_o.astype(BF16), hy_w_out.reshape(HY_NCT, HY_CT, d).astype(BF16), w_mix.astype(BF16))


def _mem_kv_kernel(m_ref, g_ref, w_ref, o_ref):
    o_ref[0] = _dot(_rms(m_ref[0], g_ref[...]).astype(BF16), w_ref[...]).astype(BF16)


def _mem_kv(mem, mem_norm, w_kv):
    b, m, d = mem.shape
    n = w_kv.shape[1]
    return pl.pallas_call(
        _mem_kv_kernel,
        grid=(b,),
        in_specs=[pl.BlockSpec((1, m, d), lambda i: (i, 0, 0)),
                  pl.BlockSpec((1, d), lambda i: (0, 0)),
                  pl.BlockSpec((d, n), lambda i: (0, 0))],
        out_specs=pl.BlockSpec((1, m, n), lambda i: (i, 0, 0)),
        out_shape=jax.ShapeDtypeStruct((b, m, n), BF16),
        compiler_params=_cparams(("parallel",)),
        name="mem_kv",
    )(mem, mem_norm, w_kv.astype(BF16))


def _xattn_kernel(x_ref, g_ref, kv_ref, wq_ref, wo_ref, o_ref):
    x = x_ref[0]
    h = _rms(x, g_ref[...]).astype(BF16)
    q = (_dot(h, wq_ref[...]) * (XA_DH ** -0.5)).astype(BF16)
    outs = []
    for hd in range(XA_HEADS):
        k = kv_ref[0, :, hd * 2 * XA_DH:hd * 2 * XA_DH + XA_DH]
        v = kv_ref[0, :, hd * 2 * XA_DH + XA_DH:(hd + 1) * 2 * XA_DH]
        s = lax.dot_general(q[:, hd * XA_DH:(hd + 1) * XA_DH], k, (((1,), (1,)), ((), ())),
                            preferred_element_type=F32)
        e = jnp.exp(s - jnp.max(s, axis=-1, keepdims=True))
        p = (e / jnp.sum(e, axis=-1, keepdims=True)).astype(BF16)
        outs.append(_dot(p, v))
    o = jnp.concatenate(outs, axis=-1).astype(BF16)
    o_ref[0] = x + _dot(o, wo_ref[...])


def _xattn(x, kv, xa_norm, w_q, w_o, *, tm=512):
    b, l, d = x.shape
    m, n = kv.shape[1], kv.shape[2]
    full = lambda s: pl.BlockSpec(s, lambda i, j: tuple(0 for _ in s))
    return pl.pallas_call(
        _xattn_kernel,
        grid=(b, l // tm),
        in_specs=[pl.BlockSpec((1, tm, d), lambda i, j: (i, j, 0)), full((1, d)),
                  pl.BlockSpec((1, m, n), lambda i, j: (i, 0, 0)),
                  full((d, XA_HEADS * XA_DH)), full((XA_HEADS * XA_DH, d))],
        out_specs=pl.BlockSpec((1, tm, d), lambda i, j: (i, j, 0)),
        out_shape=jax.ShapeDtypeStruct((b, l, d), F32),
        compiler_params=_cparams(("parallel", "parallel")),
        name="xattn",
    )(x, xa_norm, kv, w_q.astype(BF16), w_o.astype(BF16))


def _moe_route(logits):
    neg = -jnp.inf
    big = float(1 << 20)
    lane = lax.broadcasted_iota(jnp.int32, logits.shape, 1).astype(F32)
    gl = jnp.where(lane < N_GROUPS, logits, neg)
    gmax = jnp.max(gl, axis=-1, keepdims=True)
    g_idx = jnp.min(jnp.where(gl == gmax, lane, big), axis=-1, keepdims=True)
    g_w = 1.0 / jnp.sum(jnp.exp(gl - gmax), axis=-1, keepdims=True)
    lo = N_GROUPS + g_idx * EXP_PER_GROUP
    el = jnp.where((lane >= lo) & (lane < lo + EXP_PER_GROUP), logits, neg)
    v1 = jnp.max(el, axis=-1, keepdims=True)
    i1 = jnp.min(jnp.where(el == v1, lane, big), axis=-1, keepdims=True)
    el2 = jnp.where(lane == i1, neg, el)
    v2 = jnp.max(el2, axis=-1, keepdims=True)
    i2 = jnp.min(jnp.where(el2 == v2, lane, big), axis=-1, keepdims=True)
    e2 = jnp.exp(v2 - v1)
    w1 = g_w / (1.0 + e2)
    w2 = g_w * e2 / (1.0 + e2)
    return jnp.where(lane == i1, w1, 0.0) + jnp.where(lane == i2, w2, 0.0), g_idx


MOE_TM = 1024
MOE_RB = 256


def _moe_route_kernel(x_ref, g_ref, wrh_ref, wrl_ref, br_ref, tri_ref, hs_ref, cws_ref, pos_ref, seg_ref):
    tm = x_ref.shape[0]
    h = _rms(x_ref[...], g_ref[...])
    h_hi, h_lo = _split_bf16(h)
    logits = _dot3(h_hi, h_lo, wrh_ref[...], wrl_ref[...]) + br_ref[...]
    cw, g_idx = _moe_route(logits)
    lane = lax.broadcasted_iota(jnp.int32, (tm, LANE), 1).astype(F32)
    onehot = jnp.where(lane == g_idx, 1.0, 0.0)
    before = _dot(tri_ref[...], onehot.astype(BF16))
    counts = jnp.sum(onehot, axis=0, keepdims=True)
    starts = jnp.zeros_like(counts)
    for k in range(1, N_GROUPS + 1):
        below = jnp.sum(jnp.where(lane[0:1] < k, counts, 0.0), axis=-1, keepdims=True)
        starts = starts + jnp.where(lane[0:1] == k, below, 0.0)
    pos = jnp.sum(onehot * (before + starts), axis=-1, keepdims=True)
    pos_b = jnp.broadcast_to(pos, (tm, LANE))
    pos_ref[...] = pos_b
    seg_ref[0] = jnp.broadcast_to(starts, (8, LANE)).astype(jnp.int32)
    pos_row = pos_b.T[0:1]
    row = lax.broadcasted_iota(jnp.int32, (tm, tm), 0).astype(F32)
    perm = jnp.where(row == pos_row, 1.0, 0.0).astype(BF16)
    hs_ref[...] = _dot(perm, h_hi).astype(BF16)
    cw_hi, cw_lo = _split_bf16(cw)
    cws_ref[...] = _dot(perm, cw_hi) + _dot(perm, cw_lo)


def _moe_expert_kernel(seg_ref, hs_ref, cws_ref, wg_ref, wu_ref, wd_ref, o_ref, acc_ref):
    tile, grp = pl.program_id(0), pl.program_id(1)
    tm = hs_ref.shape[0]

    @pl.when(grp == 0)
    def _():
        acc_ref[...] = jnp.zeros_like(acc_ref)

    seg_lo = seg_ref[tile * 8 + grp]
    seg_hi = seg_ref[tile * 8 + grp + 1]
    for r in range(tm // MOE_RB):
        rows = pl.ds(r * MOE_RB, MOE_RB)

        @pl.when((seg_lo < (r + 1) * MOE_RB) & (seg_hi > r * MOE_RB))
        def _():
            h = hs_ref[rows, :]
            cw = cws_ref[rows, :]
            lane = lax.broadcasted_iota(jnp.int32, cw.shape, 1)
            parts = []
            for e in range(EXP_PER_GROUP):
                hid = jax.nn.silu(_dot(h, wg_ref[0, e])) * _dot(h, wu_ref[0, e])
                col = jnp.sum(jnp.where(lane == N_GROUPS + grp * EXP_PER_GROUP + e, cw, 0.0), axis=-1, keepdims=True)
                parts.append((hid * col).astype(BF16))
            acc_ref[rows, :] += _dot(jnp.concatenate(parts, axis=-1), wd_ref[0])

    @pl.when(grp == N_GROUPS - 1)
    def _():
        o_ref[...] = acc_ref[...].astype(BF16)


def _moe_unsort_kernel(x_ref, ys_ref, pos_ref, fn_ref, o_ref, *, final_norm):
    tm = x_ref.shape[0]
    col = lax.broadcasted_iota(jnp.int32, (tm, tm), 1).astype(F32)
    unperm = jnp.where(col == pos_ref[:, 0:1], 1.0, 0.0).astype(BF16)
    y = x_ref[...] + _dot(unperm, ys_ref[...])
    if final_norm:
        y = _rms(y, fn_ref[...])
    o_ref[...] = y


def _moe(x2d, moe_norm, w_group, b_group, w_expert, b_expert, w_gate, w_up, w_down, fnorm, *, final_norm):
    t, d = x2d.shape
    tm = min(MOE_TM, t)
    nt = t // tm
    npad = LANE - N_GROUPS - N_EXPERTS
    wr = jnp.concatenate([w_group, w_expert, jnp.zeros((d, npad), F32)], axis=1)
    br = jnp.concatenate([b_group, b_expert, jnp.zeros((npad,), F32)])[None]
    wrh, wrl = _split_bf16(wr)
    ge = EXP_PER_GROUP * D_FF_E
    wg = w_gate.astype(BF16).reshape(N_GROUPS, EXP_PER_GROUP, d, D_FF_E)
    wu = w_up.astype(BF16).reshape(N_GROUPS, EXP_PER_GROUP, d, D_FF_E)
    wd = w_down.astype(BF16).reshape(N_GROUPS, ge, d)
    tri =(np.arange(tm)[:, None] > np.arange(tm)[None, :]).astype(np.float32)
    full1 = lambda s: pl.BlockSpec(s, lambda i: tuple(0 for _ in s))
    row1 = lambda w: pl.BlockSpec((tm, w), lambda i: (i, 0))
    hs, cws, pos, seg = pl.pallas_call(
        _moe_route_kernel,
        grid=(nt,),
        in_specs=[row1(d), full1((1, d)), full1((d, LANE)), full1((d, LANE)), full1((1, LANE)), full1((tm, tm))],
        out_specs=[row1(d), row1(LANE), row1(LANE), pl.BlockSpec((1, 8, LANE), lambda i: (i, 0, 0))],
        out_shape=[jax.ShapeDtypeStruct((t, d), BF16), jax.ShapeDtypeStruct((t, LANE), F32),
                   jax.ShapeDtypeStruct((t, LANE), F32), jax.ShapeDtypeStruct((nt, 8, LANE), jnp.int32)],
        compiler_params=_cparams(("parallel",)),
        name="moe_route",
    )(x2d, moe_norm, wrh, wrl, br, jnp.asarray(tri, BF16))
    seg_flat = seg[:, 0, :8].reshape(nt * 8)
    ys = pl.pallas_call(
        _moe_expert_kernel,
        grid_spec=pltpu.PrefetchScalarGridSpec(
            num_scalar_prefetch=1,
            grid=(nt, N_GROUPS),
            in_specs=[pl.BlockSpec((tm, d), lambda i, j, s: (i, 0)),
                      pl.BlockSpec((tm, LANE), lambda i, j, s: (i, 0)),
                      pl.BlockSpec((1, EXP_PER_GROUP, d, D_FF_E), lambda i, j, s: (j, 0, 0, 0)),
                      pl.BlockSpec((1, EXP_PER_GROUP, d, D_FF_E), lambda i, j, s: (j, 0, 0, 0)),
                      pl.BlockSpec((1, ge, d), lambda i, j, s: (j, 0, 0))],
            out_specs=pl.BlockSpec((tm, d), lambda i, j, s: (i, 0)),
            scratch_shapes=[pltpu.VMEM((tm, d), F32)]),
        out_shape=jax.ShapeDtypeStruct((t, d), BF16),
        compiler_params=_cparams(("parallel", "arbitrary")),
        name="moe_experts",
    )(seg_flat, hs, cws, wg, wu, wd)
    return pl.pallas_call(
        functools.partial(_moe_unsort_kernel, final_norm=final_norm),
        grid=(nt,),
        in_specs=[row1(d), row1(d), row1(LANE), full1((1, d))],
        out_specs=row1(d),
        out_shape=jax.ShapeDtypeStruct((t, d), F32),
        compiler_params=_cparams(("parallel",)),
        name="moe_unsort",
    )(x2d, ys, pos, fnorm)


def _inproj_weight(w_in):
    d = w_in.shape[0]
    kr = w_in[:, _IN_M + Q_LORA + KV_LORA:_IN_H]
    half = QK_ROPE // 2
    kr_rot = jnp.concatenate([-kr[:, half:], kr[:, :half]], axis=1)
    z64 = jnp.zeros((d, QK_NOPE), F32)
    z32 = jnp.zeros((d, LANE - QK_NOPE - QK_ROPE), F32)
    cols = [w_in[:, _IN_G:], w_in[:, _IN_A:_IN_S], w_in[:, _IN_H:_IN_G], w_in[:, _IN_S:_IN_M],
            w_in[:, _IN_M:_IN_M + Q_LORA], w_in[:, _IN_M + Q_LORA:_IN_M + Q_LORA + KV_LORA],
            z64, kr, z32, z64, kr_rot, z32]
    return jnp.concatenate(cols, axis=1).astype(BF16)


def _rope_tables(positions):
    inv_freq = 1.0 / (ROPE_BASE ** (jnp.arange(0, QK_ROPE, 2, dtype=F32) / QK_ROPE))
    ang = positions.astype(F32)[..., None] * inv_freq
    cos, sin = jnp.cos(ang), jnp.sin(ang)
    shp = positions.shape
    pad = jnp.zeros(shp + (HEAD_PAD - QK_NOPE - QK_ROPE,), F32)
    cos_t = jnp.concatenate([jnp.ones(shp + (QK_NOPE,), F32), cos, cos, pad], axis=-1)
    sin_t = jnp.concatenate([jnp.zeros(shp + (QK_NOPE,), F32), sin, sin, pad], axis=-1)
    return cos_t, sin_t


def kernel(x, mem, positions, mix_norm, w_in, gate_bias, conv_a, w_out_a, s5_lambda_re, s5_lambda_im, s5_log_step, s5_b_re, s5_b_im, s5_c_re, s5_c_im, s5_d, s5_w_glu, mla_q_norm, mla_w_q_b, mla_kv_norm, mla_w_kv_b, mla_w_o, hy_conv, hy_f_w1, hy_f_b1, hy_f_w2, hy_f_b2, hy_f_w3, hy_f_freq, hy_bias, hy_w_out, w_mix_out, xa_norm, mem_norm, xa_w_q, xa_w_kv, xa_w_o, moe_norm, moe_w_group, moe_b_group, moe_w_expert, moe_b_expert, moe_w_gate, moe_w_up, moe_w_down, final_norm):
    b, l, d = x.shape
    depth = w_in.shape[0]
    cos_t, sin_t = _rope_tables(positions)
    tables = _hy_tables(l)
    for i in range(depth):
        p3 = _inproj(x.reshape(b * l, d), mix_norm[i][None], _inproj_weight(w_in[i])).reshape(b, l, N_P)
        a_pre = _shortconv(p3, conv_a[i])
        ys = _s5(p3, _s5_operators(s5_lambda_re[i], s5_lambda_im[i], s5_log_step[i], s5_b_re[i], s5_b_im[i],
                                   s5_c_re[i], s5_c_im[i], s5_d[i]))
        wq, wkv = _mla_weights(mla_w_q_b[i], mla_w_kv_b[i])
        q, k, v = _mla_prep(p3, cos_t, sin_t, mla_q_norm[i][None], mla_kv_norm[i][None], wq, wkv)
        o_mla = _mla_attn(q, k, v)
        kf_re, kf_im = _hy_filter_spectrum(l, hy_f_w1[i], hy_f_b1[i], hy_f_w2[i], hy_f_b2[i], hy_f_w3[i],
                                           hy_f_freq[i], tables)
        z_hy = _hy_conv(p3, hy_conv[i], kf_re, kf_im, hy_bias[i], tables)
        x = _merge(x, p3, a_pre, ys, o_mla, z_hy, gate_bias[i], w_out_a[i], s5_w_glu[i], mla_w_o[i],
                   hy_w_out[i], w_mix_out[i])
        kv = _mem_kv(mem, mem_norm[None], xa_w_kv[i])
        x = _xattn(x, kv, xa_norm[i][None], xa_w_q[i], xa_w_o[i])
        x = _moe(x.reshape(b * l, d), moe_norm[i][None], moe_w_group[i], moe_b_group[i], moe_w_expert[i],
                 moe_b_expert[i], moe_w_gate[i], moe_w_up[i], moe_w_down[i], final_norm[None],
                 final_norm=(i == depth - 1)).reshape(b, l, d)
    return x
```

```python
import functools
import math

import numpy as np
import jax
import jax.numpy as jnp
from jax import lax
from jax.experimental import pallas as pl
from jax.experimental.pallas import tpu as pltpu

F32 = jnp.float32
BF16 = jnp.bfloat16
EPS = 1e-6

D_MODEL = 1024
N_BRANCH = 4
W_A = 512
W_S = 512
S5_GROUP = 16
S5_GROUPS = W_S // S5_GROUP
S5_STATE = 64
S5_CHUNK = 16
MLA_HEADS = 8
Q_LORA = 256
KV_LORA = 256
QK_NOPE = 64
QK_ROPE = 32
V_DIM = 64
ROPE_BASE = 10000.0
W_H = 512
HY_ORDER = 2
HY_EMB = 33
HY_BANDS = (HY_EMB - 1) // 2
HY_FO = 64
HY_FAST_DECAY = 0.3
HY_SLOW_DECAY = 1.5
HY_TARGET = 1e-2
XA_HEADS = 4
XA_DH = 128
N_GROUPS = 4
EXP_PER_GROUP = 8
N_EXPERTS = N_GROUPS * EXP_PER_GROUP
D_FF_E = 256

LANE = 128
HEAD_PAD = 128
VMEM_LIMIT = 56 * 1024 * 1024

OFF_G = 0
OFF_A = OFF_G + N_BRANCH * D_MODEL
OFF_H = OFF_A + 3 * W_A
OFF_S = OFF_H + 3 * W_H
OFF_CQ = OFF_S + W_S
OFF_CKV = OFF_CQ + Q_LORA
OFF_KRA = OFF_CKV + KV_LORA
OFF_KRB = OFF_KRA + LANE
N_P = OFF_KRB + LANE

_IN_A = 0
_IN_S = 3 * W_A
_IN_M = _IN_S + W_S
_IN_H = _IN_M + Q_LORA + KV_LORA + QK_ROPE
_IN_G = _IN_H + 3 * W_H

HY_N1 = 8
HY_CT = 128
HY_NCT = W_H // HY_CT


def _cparams(sem, vmem=VMEM_LIMIT):
    return pltpu.CompilerParams(dimension_semantics=sem, vmem_limit_bytes=vmem)


def _split_bf16(x):
    hi = x.astype(BF16)
    lo = (x - hi.astype(F32)).astype(BF16)
    return hi, lo


def _dot(a, b):
    return jnp.dot(a, b, preferred_element_type=F32)


def _dot3(a_hi, a_lo, b_hi, b_lo):
    return _dot(a_hi, b_hi) + (_dot(a_lo, b_hi) + _dot(a_hi, b_lo))


def _rms(x, g):
    return x * lax.rsqrt(jnp.mean(x * x, axis=-1, keepdims=True) + EPS) * g


def _inproj_kernel(x_ref, g_ref, w_ref, o_ref, h_ref):
    @pl.when(pl.program_id(1) == 0)
    def _():
        h_ref[...] = _rms(x_ref[...], g_ref[...]).astype(BF16)

    o_ref[...] = _dot(h_ref[...], w_ref[...]).astype(o_ref.dtype)


def _inproj(x2d, g, w, *, tm=1024, tn=N_P // 3):
    t = x2d.shape[0]
    return pl.pallas_call(
        _inproj_kernel,
        grid=(t // tm, N_P // tn),
        in_specs=[pl.BlockSpec((tm, D_MODEL), lambda i, j: (i, 0)),
                  pl.BlockSpec((1, D_MODEL), lambda i, j: (0, 0)),
                  pl.BlockSpec((D_MODEL, tn), lambda i, j: (0, j))],
        out_specs=pl.BlockSpec((tm, tn), lambda i, j: (i, j)),
        out_shape=jax.ShapeDtypeStruct((t, N_P), BF16),
        scratch_shapes=[pltpu.VMEM((tm, D_MODEL), BF16)],
        compiler_params=_cparams(("parallel", "arbitrary")),
        name="inproj",
    )(x2d, g, w)


def _conv3(u, w):
    n = u.shape[0]
    row = lax.broadcasted_iota(jnp.int32, u.shape, 0)
    prev = jnp.where(row == 0, 0.0, pltpu.roll(u, 1, axis=0))
    nxt = jnp.where(row == n - 1, 0.0, pltpu.roll(u, n - 1, axis=0))
    return w[0:1] * prev + w[1:2] * u + w[2:3] * nxt


def _shortconv_kernel(bg_ref, cg_ref, xi_ref, wa_ref, a_ref):
    f32 = lambda r: r[0].astype(F32)
    a_ref[0] = f32(bg_ref) * _conv3(f32(cg_ref) * f32(xi_ref), wa_ref[0])


def _shortconv(p3, conv_a):
    b, l, _ = p3.shape
    nct = W_A // LANE
    wa = conv_a.reshape(3, nct, LANE).transpose(1, 0, 2)

    def pspec(off):
        return pl.BlockSpec((1, l, LANE), lambda i, j, off=off: (i, 0, off // LANE + j))

    return pl.pallas_call(
        _shortconv_kernel,
        grid=(b, nct),
        in_specs=[pspec(OFF_A), pspec(OFF_A + W_A), pspec(OFF_A + 2 * W_A),
                  pl.BlockSpec((1, 3, LANE), lambda i, j: (j, 0, 0))],
        out_specs=pl.BlockSpec((1, l, LANE), lambda i, j: (i, 0, j)),
        out_shape=jax.ShapeDtypeStruct((b, l, W_A), F32),
        compiler_params=_cparams(("parallel", "parallel")),
        name="shortconv",
    )(p3, p3, p3, wa)


S5_GPB = LANE // S5_GROUP
S5_NLB = W_S // LANE
S5_XW = S5_CHUNK * LANE
S5_SW = S5_GPB * 2 * S5_STATE


def _s5_operators(lam_re, lam_im, log_step, b_re, b_im, c_re, c_im, d_skip):
    q, hh, g = S5_CHUNK, S5_GROUP, S5_GROUPS
    hp = lax.Precision.HIGHEST
    delta = jnp.exp(log_step)[..., None]

    def powers(exps):
        e = jnp.asarray(exps, F32)
        mag = jnp.exp((lam_re * delta)[..., None] * e)
        ang = (lam_im * delta)[..., None] * e
        return mag * jnp.cos(ang), mag * jnp.sin(ang)

    ramp = np.arange(q)
    p1r, p1i = powers([1.0])
    den = lam_re * lam_re + lam_im * lam_im
    nr, ni = p1r[..., 0] - 1.0, p1i[..., 0]
    fr = (nr * lam_re + ni * lam_im) / den
    fi = (ni * lam_re - nr * lam_im) / den
    bbr = fr[..., None] * b_re - fi[..., None] * b_im
    bbi = fr[..., None] * b_im + fi[..., None] * b_re

    def times_bbar(exps):
        p_r, p_i = powers(exps)
        return (p_r[..., None] * bbr[..., None, :] - p_i[..., None] * bbi[..., None, :],
                p_r[..., None] * bbi[..., None, :] + p_i[..., None] * bbr[..., None, :])

    mr, mi = times_bbar(ramp)
    kern = (jnp.einsum('dghp,dgpek->dgehk', c_re, mr, precision=hp)
            - jnp.einsum('dghp,dgpek->dgehk', c_im, mi, precision=hp))
    i_idx = jnp.arange(q)[:, None]
    j_idx = jnp.arange(q)[None, :]
    diff_f = jnp.clip(j_idx - i_idx, 0, q - 1)
    diff_b = jnp.clip(i_idx - j_idx, 0, q - 1)
    t_f = jnp.where((i_idx <= j_idx)[None, :, :, None, None], kern[0][:, diff_f], 0.0)
    t_b = jnp.where((i_idx >= j_idx)[None, :, :, None, None], kern[1][:, diff_b], 0.0)
    t_all = (t_f + t_b).transpose(0, 1, 4, 2, 3)
    skip = jnp.eye(q, dtype=F32)[:, None, :, None] * jnp.eye(hh, dtype=F32)[None, :, None, :]
    t_all = t_all + skip[None] * d_skip.reshape(g, 1, 1, 1, hh)
    eye_g = jnp.eye(S5_GPB, dtype=BF16)
    blk = lambda a: a.astype(BF16).reshape((S5_NLB, S5_GPB) + a.shape[1:])
    r_idx = lax.broadcasted_iota(jnp.int32, (S5_GPB, q * hh, S5_XW), 1)
    n_idx = lax.broadcasted_iota(jnp.int32, (S5_GPB, q * hh, S5_XW), 2)
    g_idx = lax.broadcasted_iota(jnp.int32, (S5_GPB, q * hh, S5_XW), 0)
    spread = (n_idx == (r_idx // hh) * LANE + g_idx * hh + r_idx % hh).astype(BF16)
    cols_big = lambda a: jnp.einsum('lgrc,gcn->lgrn', blk(a), spread, preferred_element_type=BF16)
    t_big = cols_big(t_all.reshape(g, q * hh, q * hh)).reshape(S5_NLB, S5_GPB, q, hh, S5_XW)
    t_big = t_big.transpose(0, 2, 1, 3, 4).reshape(S5_NLB, S5_XW, S5_XW)

    er, ei = times_bbar(q - 1 - ramp)
    state_in = lambda f, b: jnp.concatenate([f[0].transpose(0, 2, 3, 1), b[1].transpose(0, 2, 3, 1)], axis=-1)

    def st_big(a):
        a5 = blk(a).transpose(0, 2, 1, 3, 4)
        return (a5[:, :, :, :, None, :] * eye_g[None, None, :, None, :, None]).reshape(S5_NLB, S5_XW, S5_SW)

    w1 = jnp.concatenate([t_big, st_big(state_in(er, mr)), st_big(state_in(ei, mi))], axis=-1)

    def coef(c_r, c_i, p_r, p_i):
        cr, ci = c_r.transpose(0, 2, 1)[:, :, None, :], c_i.transpose(0, 2, 1)[:, :, None, :]
        return cr * p_r[..., None] - ci * p_i[..., None], -(cr * p_i[..., None] + ci * p_r[..., None])

    pf_r, pf_i = powers(ramp + 1)
    pb_r, pb_i = powers(q - ramp)
    f_re, f_im = coef(c_re[0], c_im[0], pf_r[0], pf_i[0])
    r_re, r_im = coef(c_re[1], c_im[1], pb_r[1], pb_i[1])
    out_big = lambda a: cols_big(a.reshape(g, 2 * S5_STATE, q * hh)).reshape(S5_NLB, S5_SW, S5_XW)
    wout = jnp.concatenate([out_big(jnp.concatenate([f_re, r_re], axis=1)),
                            out_big(jnp.concatenate([f_im, r_im], axis=1))], axis=1)

    lanes = lambda a: jnp.concatenate([a[0], a[1]], axis=-1).reshape(S5_NLB, S5_SW)
    pq_r, pq_i = powers([float(q)])
    lam = jnp.stack([lanes(pq_r[..., 0]), lanes(pq_i[..., 0])], axis=1)
    return w1, wout, lam


def _s5_kernel(u_ref, w1_ref, wo_ref, lam_ref, y_ref, u32_ref, yin_ref, sre_ref, sim_ref, xre_ref, xim_ref,
               *, n_chunks):
    q, gpb, half = S5_CHUNK, S5_GPB, S5_STATE
    u32_ref[...] = u_ref[0].astype(F32)
    xcat = jnp.concatenate([u32_ref[pl.ds(i, n_chunks, stride=q), :].astype(BF16) for i in range(q)], axis=-1)
    m1 = _dot(xcat, w1_ref[0])
    yin_ref[...] = m1[:, :S5_XW]
    for g in range(gpb):
        sre_ref[pl.ds(g, n_chunks, stride=gpb), :] = m1[:, S5_XW + g * LANE:S5_XW + (g + 1) * LANE]
        sim_ref[pl.ds(g, n_chunks, stride=gpb), :] = m1[:, S5_XW + S5_SW + g * LANE:S5_XW + S5_SW + (g + 1) * LANE]
    lr = lam_ref[0, 0]
    li = lam_ref[0, 1]
    fwd_lane = lax.broadcasted_iota(jnp.int32, (gpb, LANE), 1) < half

    def step(k, carry):
        xr, xi = carry
        rf = pl.multiple_of(k * gpb, gpb)
        rb = pl.multiple_of((n_chunks - 1 - k) * gpb, gpb)
        xre_ref[pl.ds(rf, gpb), 0:half] = xr[:, 0:half]
        xim_ref[pl.ds(rf, gpb), 0:half] = xi[:, 0:half]
        xre_ref[pl.ds(rb, gpb), half:LANE] = xr[:, half:LANE]
        xim_ref[pl.ds(rb, gpb), half:LANE] = xi[:, half:LANE]
        ar = jnp.where(fwd_lane, sre_ref[pl.ds(rf, gpb), :], sre_ref[pl.ds(rb, gpb), :])
        ai = jnp.where(fwd_lane, sim_ref[pl.ds(rf, gpb), :], sim_ref[pl.ds(rb, gpb), :])
        return lr * xr - li * xi + ar, lr * xi + li * xr + ai

    zero = jnp.zeros((gpb, LANE), F32)
    lax.fori_loop(0, n_chunks, step, (zero, zero))
    xs = jnp.concatenate([r[pl.ds(g, n_chunks, stride=gpb), :].astype(BF16)
                          for r in (xre_ref, xim_ref) for g in range(gpb)], axis=-1)
    y = yin_ref[...] + _dot(xs, wo_ref[0])
    for j in range(q):
        y_ref[0, pl.ds(j, n_chunks, stride=q), :] = y[:, j * LANE:(j + 1) * LANE]


def _s5(p3, ops):
    b, l, _ = p3.shape
    nc = l // S5_CHUNK
    w1, wout, lam = ops
    lam = lam.reshape(S5_NLB, 2, S5_GPB, LANE)
    once = dict(pipeline_mode=pl.Buffered(1))
    return pl.pallas_call(
        functools.partial(_s5_kernel, n_chunks=nc),
        grid=(S5_NLB, b),
        in_specs=[pl.BlockSpec((1, l, LANE), lambda j, i: (i, 0, OFF_S // LANE + j)),
                  pl.BlockSpec((1, S5_XW, S5_XW + 2 * S5_SW), lambda j, i: (j, 0, 0), **once),
                  pl.BlockSpec((1, 2 * S5_SW, S5_XW), lambda j, i: (j, 0, 0), **once),
                  pl.BlockSpec((1, 2, S5_GPB, LANE), lambda j, i: (j, 0, 0, 0))],
        out_specs=pl.BlockSpec((1, l, LANE), lambda j, i: (i, 0, j)),
        out_shape=jax.ShapeDtypeStruct((b, l, W_S), F32),
        scratch_shapes=[pltpu.VMEM((l, LANE), F32), pltpu.VMEM((nc, S5_XW), F32)]
        + [pltpu.VMEM((nc * S5_GPB, LANE), F32)] * 4,
        compiler_params=_cparams(("parallel", "arbitrary")),
        name="s5_scan",
    )(p3, w1, wout, lam)


def _mla_prep_kernel(cq_ref, ckv_ref, kra_ref, krb_ref, cos_ref, sin_ref, qn_ref, kvn_ref,
                     wq_ref, wkv_ref, q_ref, k_ref, v_ref):
    cqn = _rms(cq_ref[0].astype(F32), qn_ref[...]).astype(BF16)
    ckvn = _rms(ckv_ref[0].astype(F32), kvn_ref[...]).astype(BF16)
    cos = cos_ref[0]
    sin = sin_ref[0]
    kr = kra_ref[0].astype(F32) * cos + krb_ref[0].astype(F32) * sin
    lane = lax.broadcasted_iota(jnp.int32, cos.shape, 1)
    ones_col = jnp.where(lane == V_DIM, 1.0, 0.0)
    scale = (QK_NOPE + QK_ROPE) ** -0.5
    for h in range(MLA_HEADS):
        qq = _dot(cqn, wq_ref[h])
        q_ref[0, h] = ((qq[:, :HEAD_PAD] * cos + qq[:, HEAD_PAD:] * sin) * scale).astype(BF16)
        kv = _dot(ckvn, wkv_ref[h])
        k_ref[0, h] = (kv[:, :HEAD_PAD] + kr).astype(BF16)
        v_ref[0, h] = (kv[:, HEAD_PAD:] + ones_col).astype(BF16)


def _mla_weights(w_q_b, w_kv_b):
    dq = QK_NOPE + QK_ROPE
    half = QK_ROPE // 2
    wq = w_q_b.reshape(Q_LORA, MLA_HEADS, dq).transpose(1, 0, 2)
    rope = wq[..., QK_NOPE:]
    rot = jnp.concatenate([-rope[..., half:], rope[..., :half]], axis=-1)
    zq = jnp.zeros((MLA_HEADS, Q_LORA, HEAD_PAD - dq), F32)
    zn = jnp.zeros((MLA_HEADS, Q_LORA, QK_NOPE), F32)
    wq_full = jnp.concatenate([wq, zq, zn, rot, zq], axis=-1)
    wkv = w_kv_b.reshape(KV_LORA, MLA_HEADS, QK_NOPE + V_DIM).transpose(1, 0, 2)
    zk = jnp.zeros((MLA_HEADS, KV_LORA, HEAD_PAD - QK_NOPE), F32)
    zv = jnp.zeros((MLA_HEADS, KV_LORA, HEAD_PAD - V_DIM), F32)
    wkv_full = jnp.concatenate([wkv[..., :QK_NOPE], zk, wkv[..., QK_NOPE:], zv], axis=-1)
    return wq_full.astype(BF16), wkv_full.astype(BF16)


def _mla_prep(p3, cos_t, sin_t, q_norm, kv_norm, wq, wkv, *, tl=512):
    b, l, _ = p3.shape
    hspec = pl.BlockSpec((1, MLA_HEADS, tl, HEAD_PAD), lambda i, j: (i, 0, j, 0))
    hshape = jax.ShapeDtypeStruct((b, MLA_HEADS, l, HEAD_PAD), BF16)
    tab = pl.BlockSpec((1, tl, LANE), lambda i, j: (i, j, 0))
    return pl.pallas_call(
        _mla_prep_kernel,
        grid=(b, l // tl),
        in_specs=[pl.BlockSpec((1, tl, Q_LORA), lambda i, j: (i, j, OFF_CQ // Q_LORA)),
                  pl.BlockSpec((1, tl, KV_LORA), lambda i, j: (i, j, OFF_CKV // KV_LORA)),
                  pl.BlockSpec((1, tl, LANE), lambda i, j: (i, j, OFF_KRA // LANE)),
                  pl.BlockSpec((1, tl, LANE), lambda i, j: (i, j, OFF_KRB // LANE)),
                  tab, tab,
                  pl.BlockSpec((1, Q_LORA), lambda i, j: (0, 0)),
                  pl.BlockSpec((1, KV_LORA), lambda i, j: (0, 0)),
                  pl.BlockSpec((MLA_HEADS, Q_LORA, 2 * HEAD_PAD), lambda i, j: (0, 0, 0)),
                  pl.BlockSpec((MLA_HEADS, KV_LORA, 2 * HEAD_PAD), lambda i, j: (0, 0, 0))],
        out_specs=[hspec, hspec, hspec],
        out_shape=[hshape, hshape, hshape],
        compiler_params=_cparams(("parallel", "parallel")),
        name="mla_prep",
    )(p3, p3, p3, p3, cos_t, sin_t, q_norm, kv_norm, wq, wkv)


def _mla_attn_kernel(q_ref, k_ref, v_ref, o_ref, *, tq):
    def q_tile(t, carry):
        rows = pl.ds(pl.multiple_of(t * tq, tq), tq)
        outs = []
        for h in range(2):
            s = lax.dot_general(q_ref[0, h, rows, :], k_ref[0, h], (((1,), (1,)), ((), ())),
                                preferred_element_type=F32)
            m = jnp.max(s, axis=-1, keepdims=True)
            p = jnp.exp(s - m).astype(BF16)
            o = _dot(p, v_ref[0, h])
            outs.append(o / o[:, V_DIM:V_DIM + 1])
        lane = lax.broadcasted_iota(jnp.int32, outs[0].shape, 1)
        o_ref[0, rows, :] = jnp.where(lane < V_DIM, outs[0], pltpu.roll(outs[1], V_DIM, axis=1))
        return carry

    lax.fori_loop(0, q_ref.shape[2] // tq, q_tile, 0)


def _mla_attn(q, k, v, *, tq=256):
    b, _, l, _ = q.shape
    hspec = pl.BlockSpec((1, 2, l, HEAD_PAD), lambda i, j: (i, j, 0, 0))
    return pl.pallas_call(
        functools.partial(_mla_attn_kernel, tq=tq),
        grid=(b, MLA_HEADS // 2),
        in_specs=[hspec, hspec, hspec],
        out_specs=pl.BlockSpec((1, l, 2 * V_DIM), lambda i, j: (i, 0, j)),
        out_shape=jax.ShapeDtypeStruct((b, l, MLA_HEADS * V_DIM), F32),
        compiler_params=_cparams(("parallel", "parallel")),
        name="mla_attn",
    )(q, k, v)


def _hy_sizes(l):
    n = 2 * l
    n2 = n // HY_N1
    nf = n2 // 2 + 1
    nfp = ((nf + 63) // 64) * 64
    return n, n2, nf, nfp


def _hy_tables(l):
    n, n2, nf, nfp = _hy_sizes(l)
    f2 = np.arange(nfp)[:, None].astype(np.float64)
    t2 = np.arange(n2 // 2)[None, :].astype(np.float64)
    valid = (np.arange(nfp) < nf)[:, None]
    ang = 2.0 * np.pi * f2 * t2 / n2
    fwd = np.concatenate([np.where(valid, np.cos(ang), 0.0), np.where(valid, -np.sin(ang), 0.0)], axis=0)
    wgt = np.where((np.arange(nfp) == 0) | (np.arange(nfp) == nf - 1), 1.0, 2.0)[:, None] * valid / n
    inv = np.concatenate([(wgt * np.cos(ang)).T, (-wgt * np.sin(ang)).T], axis=1)
    t1 = np.arange(HY_N1)[None, :].astype(np.float64)
    tw_ang = 2.0 * np.pi * f2 * t1 / n
    tw_re = np.repeat(np.cos(tw_ang), HY_CT, axis=1).astype(np.float32)
    tw_im = np.repeat(-np.sin(tw_ang), HY_CT, axis=1).astype(np.float32)
    fwd_hi, fwd_lo = _split_bf16(jnp.asarray(fwd, F32))
    return fwd_hi, fwd_lo, jnp.asarray(inv, F32).astype(BF16), jnp.asarray(tw_re), jnp.asarray(tw_im)


def _cmul(a, b):
    return a[0] * b[0] - a[1] * b[1], a[0] * b[1] + a[1] * b[0]


def _cadd(a, b):
    return a[0] + b[0], a[1] + b[1]


def _csub(a, b):
    return a[0] - b[0], a[1] - b[1]


def _cmul_i(a, sign):
    return (-a[1], a[0]) if sign > 0 else (a[1], -a[0])


def _fft4(a, sign):
    s0, s1 = _cadd(a[0], a[2]), _csub(a[0], a[2])
    s2, s3 = _cadd(a[1], a[3]), _csub(a[1], a[3])
    r3 = _cmul_i(s3, sign)
    return [_cadd(s0, s2), _cadd(s1, r3), _csub(s0, s2), _csub(s1, r3)]


def _fft8(x, sign):
    e = _fft4([x[0], x[2], x[4], x[6]], sign)
    o = _fft4([x[1], x[3], x[5], x[7]], sign)
    r = math.sqrt(0.5)
    o1 = ((o[1][0] - sign * o[1][1]) * r, (o[1][1] + sign * o[1][0]) * r)
    o2 = _cmul_i(o[2], sign)
    o3 = ((-o[3][0] - sign * o[3][1]) * r, (-o[3][1] + sign * o[3][0]) * r)
    tw = [o[0], o1, o2, o3]
    return [_cadd(e[k], tw[k]) for k in range(4)] + [_csub(e[k], tw[k]) for k in range(4)]


def _blocks(ref_re, ref_im, rows):
    return [(ref_re[rows, k * HY_CT:(k + 1) * HY_CT], ref_im[rows, k * HY_CT:(k + 1) * HY_CT])
            for k in range(HY_N1)]


def _hy_spectrum(z_ref, twr_ref, twi_ref, rows, nfp):
    t = []
    for k in range(HY_N1):
        sl = slice(k * HY_CT, (k + 1) * HY_CT)
        zk = (z_ref[rows, sl], z_ref[pl.ds(pl.multiple_of(nfp + rows.start, rows.size), rows.size), sl])
        t.append(_cmul(zk, (twr_ref[rows, sl], twi_ref[rows, sl])))
    return _fft8(t, -1)


def _hy_fold(nat_ref, rows):
    return jnp.concatenate([nat_ref[pl.ds(t1, rows, stride=HY_N1), :] for t1 in range(HY_N1)], axis=-1)


def _hy_filter_kernel(z_ref, w1_ref, b1_ref, w2_ref, b2_ref, w3_ref, fr_ref, tn_ref, dl_ref,
                      fh_ref, fl_ref, twr_ref, twi_ref, kr_ref, ki_ref, ff_ref, fb_ref, zf_ref, zb_ref,
                      *, nfp, row_chunk, rows):
    hp = lax.Precision.HIGHEST
    fr = fr_ref[...]
    h = jnp.sin(fr * (jnp.dot(z_ref[...], w1_ref[...], precision=hp, preferred_element_type=F32) + b1_ref[...]))
    h = jnp.sin(fr * (jnp.dot(h, w2_ref[...], precision=hp, preferred_element_type=F32) + b2_ref[...]))
    filt = jnp.dot(h, w3_ref[0, 0], precision=hp, preferred_element_type=F32)
    decay = jnp.exp(-tn_ref[...] * dl_ref[0])
    fwd = filt[:, :HY_CT] * decay
    row = lax.broadcasted_iota(jnp.int32, fwd.shape, 0)
    bwd = jnp.where(row == 0, 0.0, filt[:, HY_CT:] * decay)
    inv = lax.rsqrt(jnp.sum(fwd * fwd, axis=0, keepdims=True) + jnp.sum(bwd * bwd, axis=0, keepdims=True) + EPS)
    ff_ref[...] = fwd
    fb_ref[...] = bwd
    for src, dst in ((ff_ref, zf_ref), (fb_ref, zb_ref)):
        k_hi, k_lo = _split_bf16(_hy_fold(src, rows))
        dst[...] = _dot3(fh_ref[...], fl_ref[...], k_hi, k_lo)
    def chunk(c, carry):
        rws = pl.ds(pl.multiple_of(c * row_chunk, row_chunk), row_chunk)
        sf = _hy_spectrum(zf_ref, twr_ref, twi_ref, rws, nfp)
        sb = _hy_spectrum(zb_ref, twr_ref, twi_ref, rws, nfp)
        for f1 in range(HY_N1):
            sl = slice(f1 * HY_CT, (f1 + 1) * HY_CT)
            kr_ref[0, 0, rws, sl] = ((sf[f1][0] + sb[f1][0]) * inv).astype(BF16)
            ki_ref[0, 0, rws, sl] = ((sf[f1][1] - sb[f1][1]) * inv).astype(BF16)
        return carry

    lax.fori_loop(0, nfp // row_chunk, chunk, 0)


HY_ROW_CHUNK = 16


def _hy_row_chunk(nfp):
    assert nfp % HY_ROW_CHUNK == 0
    return HY_ROW_CHUNK


def _hy_conv_chunk(nfp):
    return next(c for c in (96, 64, 48, 32, 16) if nfp % c == 0)


def _hy_filter_spectrum(l, w1, b1, w2, b2, w3, freq, tables):
    n, n2, nf, nfp = _hy_sizes(l)
    fwd_hi, fwd_lo, _, tw_re, tw_im = tables
    rows = l // HY_N1
    wide = HY_N1 * HY_CT
    t = np.arange(l, dtype=np.float32)
    t_norm = t / np.float32(max(l - 1, 1))
    bands = np.linspace(1e-4, HY_BANDS - 1, HY_BANDS, dtype=np.float32)
    ang = np.float32(2.0 * math.pi / l) * t[:, None] * bands[None]
    z = np.concatenate([t_norm[:, None], np.cos(ang), -np.sin(ang)], axis=-1).astype(np.float32)
    kpad = 40
    z = np.pad(z, ((0, 0), (0, kpad - HY_EMB)))
    w1p = jnp.pad(w1, ((0, kpad - HY_EMB), (0, 0)))
    max_decay = math.log(HY_TARGET) / HY_FAST_DECAY
    min_decay = math.log(HY_TARGET) / HY_SLOW_DECAY
    deltas = np.abs(np.linspace(min_decay, max_decay, W_H, dtype=np.float32)).reshape(HY_NCT, 1, HY_CT)
    w3t = w3.reshape(HY_FO, HY_ORDER, 2, HY_NCT, HY_CT).transpose(1, 3, 0, 2, 4).reshape(HY_ORDER, HY_NCT, HY_FO, 2 * HY_CT)
    full = lambda s: pl.BlockSpec(s, lambda i, j: (0, 0))
    ospec = pl.BlockSpec((1, 1, nfp, wide), lambda i, j: (i, j, 0, 0))
    oshape = jax.ShapeDtypeStruct((HY_ORDER, HY_NCT, nfp, wide), BF16)
    return pl.pallas_call(
        functools.partial(_hy_filter_kernel, nfp=nfp, row_chunk=_hy_row_chunk(nfp), rows=rows),
        grid=(HY_ORDER, HY_NCT),
        in_specs=[full((l, kpad)), full((kpad, HY_FO)), full((1, HY_FO)), full((HY_FO, HY_FO)), full((1, HY_FO)),
                  pl.BlockSpec((1, 1, HY_FO, 2 * HY_CT), lambda i, j: (i, j, 0, 0)),
                  full((1, HY_FO)), full((l, 1)),
                  pl.BlockSpec((1, 1, HY_CT), lambda i, j: (j, 0, 0)),
                  full((2 * nfp, rows)), full((2 * nfp, rows)), full((nfp, wide)), full((nfp, wide))],
        out_specs=[ospec, ospec],
        out_shape=[oshape, oshape],
        scratch_shapes=[pltpu.VMEM((l, HY_CT), F32), pltpu.VMEM((l, HY_CT), F32),
                        pltpu.VMEM((2 * nfp, wide), F32), pltpu.VMEM((2 * nfp, wide), F32)],
        compiler_params=_cparams(("parallel", "parallel")),
        name="hyena_filter",
    )(jnp.asarray(z), w1p, b1[None], w2, b2[None], w3t, freq[None], jnp.asarray(t_norm[:, None]),
      jnp.asarray(deltas), fwd_hi, fwd_lo, tw_re, tw_im)


def _hy_conv_kernel(pv_ref, p1_ref, p2_ref, wc_ref, kr_ref, ki_ref, bias_ref, fwd_ref, inv_ref, twr_ref, twi_ref,
                    o_ref, nat_ref, z_ref, u_ref, *, nfp, row_chunk, rows):
    def folded_conv3(p_ref, part):
        nat_ref[...] = _conv3(p_ref[0].astype(F32), wc_ref[0, part])
        return _hy_fold(nat_ref, rows)

    def long_conv(u, order):
        z_ref[...] = _dot(fwd_ref[...], u.astype(BF16)).astype(BF16)
        for c in range(nfp // row_chunk):
            rws = pl.ds(c * row_chunk, row_chunk)
            spec = _hy_spectrum(z_ref, twr_ref, twi_ref, rws, nfp)
            kf = _blocks(kr_ref.at[order, 0], ki_ref.at[order, 0], rws)
            y = _fft8([_cmul(spec[f1], kf[f1]) for f1 in range(HY_N1)], +1)
            for t1 in range(HY_N1):
                sl = slice(t1 * HY_CT, (t1 + 1) * HY_CT)
                w = _cmul(y[t1], (twr_ref[rws, sl], -twi_ref[rws, sl]))
                u_ref[rws, sl] = w[0]
                u_ref[pl.ds(nfp + c * row_chunk, row_chunk), sl] = w[1]
        return _dot(inv_ref[...], u_ref[...])

    v = folded_conv3(pv_ref, 0)
    x1 = folded_conv3(p1_ref, 1)
    x2 = folded_conv3(p2_ref, 2)
    bias = bias_ref[0]
    z1 = x1 * (long_conv(v, 0) + v * bias[0:1])
    z2 = x2 * (long_conv(z1, 1) + z1 * bias[1:2])
    for t1 in range(HY_N1):
        o_ref[0, pl.ds(t1, rows, stride=HY_N1), :] = z2[:, t1 * HY_CT:(t1 + 1) * HY_CT]


def _hy_conv(p3, hy_conv, kf_re, kf_im, bias, tables):
    b, l, _ = p3.shape
    n, n2, nf, nfp = _hy_sizes(l)
    fwd_hi, _, inv_t, tw_re, tw_im = tables
    rows = l // HY_N1
    wide = HY_N1 * HY_CT
    wc = hy_conv.reshape(3, 3, HY_NCT, HY_CT).transpose(2, 1, 0, 3)
    bias_t = jnp.tile(bias.reshape(HY_ORDER, HY_NCT, 1, HY_CT), (1, 1, HY_N1, 1))
    bias_t = bias_t.transpose(1, 0, 2, 3).reshape(HY_NCT, HY_ORDER, wide)
    once = dict(pipeline_mode=pl.Buffered(1))

    def pspec(part):
        return pl.BlockSpec((1, l, HY_CT), lambda j, i, part=part: (i, 0, (OFF_H + part * W_H) // HY_CT + j))

    kspec = pl.BlockSpec((HY_ORDER, 1, nfp, wide), lambda j, i: (0, j, 0, 0), **once)
    full = lambda s: pl.BlockSpec(s, lambda j, i: (0, 0), **once)
    return pl.pallas_call(
        functools.partial(_hy_conv_kernel, nfp=nfp, row_chunk=_hy_conv_chunk(nfp), rows=rows),
        grid=(HY_NCT, b),
        in_specs=[pspec(0), pspec(1), pspec(2),
                  pl.BlockSpec((1, 3, 3, HY_CT), lambda j, i: (j, 0, 0, 0)),
                  kspec, kspec,
                  pl.BlockSpec((1, HY_ORDER, wide), lambda j, i: (j, 0, 0)),
                  full((2 * nfp, rows)), full((rows, 2 * nfp)), full((nfp, wide)), full((nfp, wide))],
        out_specs=pl.BlockSpec((1, l, HY_CT), lambda j, i: (i, 0, j)),
        out_shape=jax.ShapeDtypeStruct((b, l, W_H), F32),
        scratch_shapes=[pltpu.VMEM((l, HY_CT), F32), pltpu.VMEM((2 * nfp, wide), BF16),
                        pltpu.VMEM((2 * nfp, wide), BF16)],
        compiler_params=_cparams(("parallel", "arbitrary")),
        name="hyena_conv",
    )(p3, p3, p3, wc, kf_re, kf_im, bias_t, fwd_hi, inv_t, tw_re.astype(BF16), tw_im.astype(BF16))


def _merge_kernel(x_ref, pg_ref, a_ref, ys_ref, om_ref, zh_ref, gb_ref, wa_ref, wglu_ref, wo_ref,
                  wh_ref, wmix_ref, o_ref):
    d = D_MODEL
    y_a = _dot(a_ref[0].astype(BF16), wa_ref[...])
    glu = _dot(jax.nn.gelu(ys_ref[0]).astype(BF16), wglu_ref[...])
    y_s = glu[:, :d] * jax.nn.sigmoid(glu[:, d:])
    y_m = _dot(om_ref[0].astype(BF16), wo_ref[...])
    y_h = _dot(zh_ref[0].astype(BF16), wh_ref[...])
    gb = gb_ref[...]
    merged = jnp.zeros_like(y_a)
    for i, y in enumerate((y_a, y_s, y_m, y_h)):
        merged = merged + jax.nn.sigmoid(pg_ref[0, :, i * d:(i + 1) * d].astype(F32) + gb[i:i + 1]) * y
    o_ref[0] = x_ref[0] + _dot(merged.astype(BF16), wmix_ref[...])


def _merge(x, p3, a_pre, ys, o_mla, z_hy, gate_bias, w_out_a, w_glu, w_o, hy_w_out, w_mix, *, tm=512):
    b, l, d = x.shape
    row = lambda w: pl.BlockSpec((1, tm, w), lambda i, j: (i, j, 0))
    full = lambda s: pl.BlockSpec(s, lambda i, j: tuple(0 for _ in s))
    return pl.pallas_call(
        _merge_kernel,
        grid=(b, l // tm),
        in_specs=[row(d), row(N_BRANCH * d), row(W_A), row(W_S), row(MLA_HEADS * V_DIM), row(W_H),
                  full((N_BRANCH, d)), full((W_A, d)), full((W_S, 2 * d)), full((MLA_HEADS * V_DIM, d)),
                  full((W_H, d)), full((d, d))],
        out_specs=row(d),
        out_shape=jax.ShapeDtypeStruct((b, l, d), F32),
        compiler_params=_cparams(("parallel", "parallel")),
        name="merge",
    )(x, p3, a_pre, ys, o_mla, z_hy, gate_bias, w_out_a.astype(BF16), w_glu.astype(BF16),
      w_o.astype(BF16), hy_w_out.astype(BF16), w_mix.astype(BF16))


def _mem_kv_kernel(m_ref, g_ref, w_ref, o_ref):
    o_ref[0] = _dot(_rms(m_ref[0], g_ref[...]).astype(BF16), w_ref[...]).astype(BF16)


def _mem_kv(mem, mem_norm, w_kv):
    b, m, d = mem.shape
    n = w_kv.shape[1]
    return pl.pallas_call(
        _mem_kv_kernel,
        grid=(b,),
        in_specs=[pl.BlockSpec((1, m, d), lambda i: (i, 0, 0)),
                  pl.BlockSpec((1, d), lambda i: (0, 0)),
                  pl.BlockSpec((d, n), lambda i: (0, 0))],
        out_specs=pl.BlockSpec((1, m, n), lambda i: (i, 0, 0)),
        out_shape=jax.ShapeDtypeStruct((b, m, n), BF16),
        compiler_params=_cparams(("parallel",)),
        name="mem_kv",
    )(mem, mem_norm, w_kv.astype(BF16))


def _xattn_kernel(x_ref, g_ref, kv_ref, wq_ref, wo_ref, o_ref):
    x = x_ref[0]
    h = _rms(x, g_ref[...]).astype(BF16)
    q = (_dot(h, wq_ref[...]) * (XA_DH ** -0.5)).astype(BF16)
    outs = []
    for hd in range(XA_HEADS):
        k = kv_ref[0, :, hd * 2 * XA_DH:hd * 2 * XA_DH + XA_DH]
        v = kv_ref[0, :, hd * 2 * XA_DH + XA_DH:(hd + 1) * 2 * XA_DH]
        s = lax.dot_general(q[:, hd * XA_DH:(hd + 1) * XA_DH], k, (((1,), (1,)), ((), ())),
                            preferred_element_type=F32)
        e = jnp.exp(s - jnp.max(s, axis=-1, keepdims=True))
        p = (e / jnp.sum(e, axis=-1, keepdims=True)).astype(BF16)
        outs.append(_dot(p, v))
    o = jnp.concatenate(outs, axis=-1).astype(BF16)
    o_ref[0] = x + _dot(o, wo_ref[...])


def _xattn(x, kv, xa_norm, w_q, w_o, *, tm=512):
    b, l, d = x.shape
    m, n = kv.shape[1], kv.shape[2]
    full = lambda s: pl.BlockSpec(s, lambda i, j: tuple(0 for _ in s))
    return pl.pallas_call(
        _xattn_kernel,
        grid=(b, l // tm),
        in_specs=[pl.BlockSpec((1, tm, d), lambda i, j: (i, j, 0)), full((1, d)),
                  pl.BlockSpec((1, m, n), lambda i, j: (i, 0, 0)),
                  full((d, XA_HEADS * XA_DH)), full((XA_HEADS * XA_DH, d))],
        out_specs=pl.BlockSpec((1, tm, d), lambda i, j: (i, j, 0)),
        out_shape=jax.ShapeDtypeStruct((b, l, d), F32),
        compiler_params=_cparams(("parallel", "parallel")),
        name="xattn",
    )(x, xa_norm, kv, w_q.astype(BF16), w_o.astype(BF16))


def _moe_route(logits):
    neg = -jnp.inf
    big = float(1 << 20)
    lane = lax.broadcasted_iota(jnp.int32, logits.shape, 1).astype(F32)
    gl = jnp.where(lane < N_GROUPS, logits, neg)
    gmax = jnp.max(gl, axis=-1, keepdims=True)
    g_idx = jnp.min(jnp.where(gl == gmax, lane, big), axis=-1, keepdims=True)
    g_w = 1.0 / jnp.sum(jnp.exp(gl - gmax), axis=-1, keepdims=True)
    lo = N_GROUPS + g_idx * EXP_PER_GROUP
    el = jnp.where((lane >= lo) & (lane < lo + EXP_PER_GROUP), logits, neg)
    v1 = jnp.max(el, axis=-1, keepdims=True)
    i1 = jnp.min(jnp.where(el == v1, lane, big), axis=-1, keepdims=True)
    el2 = jnp.where(lane == i1, neg, el)
    v2 = jnp.max(el2, axis=-1, keepdims=True)
    i2 = jnp.min(jnp.where(el2 == v2, lane, big), axis=-1, keepdims=True)
    e2 = jnp.exp(v2 - v1)
    w1 = g_w / (1.0 + e2)
    w2 = g_w * e2 / (1.0 + e2)
    return jnp.where(lane == i1, w1, 0.0) + jnp.where(lane == i2, w2, 0.0), g_idx


MOE_TM = 1024
MOE_RB = 128


def _moe_route_kernel(x_ref, g_ref, wrh_ref, wrl_ref, br_ref, tri_ref, hs_ref, cws_ref, pos_ref, seg_ref):
    tm = x_ref.shape[0]
    h = _rms(x_ref[...], g_ref[...])
    h_hi, h_lo = _split_bf16(h)
    logits = _dot3(h_hi, h_lo, wrh_ref[...], wrl_ref[...]) + br_ref[...]
    cw, g_idx = _moe_route(logits)
    lane = lax.broadcasted_iota(jnp.int32, (tm, LANE), 1).astype(F32)
    onehot = jnp.where(lane == g_idx, 1.0, 0.0)
    before = _dot(tri_ref[...], onehot.astype(BF16))
    counts = jnp.sum(onehot, axis=0, keepdims=True)
    starts = jnp.zeros_like(counts)
    for k in range(1, N_GROUPS + 1):
        below = jnp.sum(jnp.where(lane[0:1] < k, counts, 0.0), axis=-1, keepdims=True)
        starts = starts + jnp.where(lane[0:1] == k, below, 0.0)
    pos = jnp.sum(onehot * (before + starts), axis=-1, keepdims=True)
    pos_b = jnp.broadcast_to(pos, (tm, LANE))
    pos_ref[...] = pos_b
    seg_ref[0] = jnp.broadcast_to(starts, (8, LANE)).astype(jnp.int32)
    pos_row = pos_b.T[0:1]
    row = lax.broadcasted_iota(jnp.int32, (tm, tm), 0).astype(F32)
    perm = jnp.where(row == pos_row, 1.0, 0.0).astype(BF16)
    hs_ref[...] = _dot(perm, h_hi).astype(BF16)
    cw_hi, cw_lo = _split_bf16(cw)
    cws_ref[...] = _dot(perm, cw_hi) + _dot(perm, cw_lo)


def _moe_expert_kernel(seg_ref, hs_ref, cws_ref, wg_ref, wu_ref, wd_ref, o_ref, acc_ref):
    tile, grp = pl.program_id(0), pl.program_id(1)
    tm = hs_ref.shape[0]

    @pl.when(grp == 0)
    def _():
        acc_ref[...] = jnp.zeros_like(acc_ref)

    seg_lo = seg_ref[tile * 8 + grp]
    seg_hi = seg_ref[tile * 8 + grp + 1]
    for r in range(tm // MOE_RB):
        rows = pl.ds(r * MOE_RB, MOE_RB)

        @pl.when((seg_lo < (r + 1) * MOE_RB) & (seg_hi > r * MOE_RB))
        def _():
            h = hs_ref[rows, :]
            cw = cws_ref[rows, :]
            lane = lax.broadcasted_iota(jnp.int32, cw.shape, 1)
            parts = []
            for e in range(EXP_PER_GROUP):
                hid = jax.nn.silu(_dot(h, wg_ref[0, e])) * _dot(h, wu_ref[0, e])
                col = jnp.sum(jnp.where(lane == N_GROUPS + grp * EXP_PER_GROUP + e, cw, 0.0), axis=-1, keepdims=True)
                parts.append((hid * col).astype(BF16))
            acc_ref[rows, :] += _dot(jnp.concatenate(parts, axis=-1), wd_ref[0])

    @pl.when(grp == N_GROUPS - 1)
    def _():
        o_ref[...] = acc_ref[...].astype(BF16)


def _moe_unsort_kernel(x_ref, ys_ref, pos_ref, fn_ref, o_ref, *, final_norm):
    tm = x_ref.shape[0]
    col = lax.broadcasted_iota(jnp.int32, (tm, tm), 1).astype(F32)
    unperm = jnp.where(col == pos_ref[:, 0:1], 1.0, 0.0).astype(BF16)
    y = x_ref[...] + _dot(unperm, ys_ref[...])
    if final_norm:
        y = _rms(y, fn_ref[...])
    o_ref[...] = y


def _moe(x2d, moe_norm, w_group, b_group, w_expert, b_expert, w_gate, w_up, w_down, fnorm, *, final_norm):
    t, d = x2d.shape
    tm = min(MOE_TM, t)
    nt = t // tm
    npad = LANE - N_GROUPS - N_EXPERTS
    wr = jnp.concatenate([w_group, w_expert, jnp.zeros((d, npad), F32)], axis=1)
    br = jnp.concatenate([b_group, b_expert, jnp.zeros((npad,), F32)])[None]
    wrh, wrl = _split_bf16(wr)
    ge = EXP_PER_GROUP * D_FF_E
    wg = w_gate.astype(BF16).reshape(N_GROUPS, EXP_PER_GROUP, d, D_FF_E)
    wu = w_up.astype(BF16).reshape(N_GROUPS, EXP_PER_GROUP, d, D_FF_E)
    wd = w_down.astype(BF16).reshape(N_GROUPS, ge, d)
    tri =(np.arange(tm)[:, None] > np.arange(tm)[None, :]).astype(np.float32)
    full1 = lambda s: pl.BlockSpec(s, lambda i: tuple(0 for _ in s))
    row1 = lambda w: pl.BlockSpec((tm, w), lambda i: (i, 0))
    hs, cws, pos, seg = pl.pallas_call(
        _moe_route_kernel,
        grid=(nt,),
        in_specs=[row1(d), full1((1, d)), full1((d, LANE)), full1((d, LANE)), full1((1, LANE)), full1((tm, tm))],
        out_specs=[row1(d), row1(LANE), row1(LANE), pl.BlockSpec((1, 8, LANE), lambda i: (i, 0, 0))],
        out_shape=[jax.ShapeDtypeStruct((t, d), BF16), jax.ShapeDtypeStruct((t, LANE), F32),
                   jax.ShapeDtypeStruct((t, LANE), F32), jax.ShapeDtypeStruct((nt, 8, LANE), jnp.int32)],
        compiler_params=_cparams(("parallel",)),
        name="moe_route",
    )(x2d, moe_norm, wrh, wrl, br, jnp.asarray(tri, BF16))
    seg_flat = seg[:, 0, :8].reshape(nt * 8)
    ys = pl.pallas_call(
        _moe_expert_kernel,
        grid_spec=pltpu.PrefetchScalarGridSpec(
            num_scalar_prefetch=1,
            grid=(nt, N_GROUPS),
            in_specs=[pl.BlockSpec((tm, d), lambda i, j, s: (i, 0)),
                      pl.BlockSpec((tm, LANE), lambda i, j, s: (i, 0)),
                      pl.BlockSpec((1, EXP_PER_GROUP, d, D_FF_E), lambda i, j, s: (j, 0, 0, 0)),
                      pl.BlockSpec((1, EXP_PER_GROUP, d, D_FF_E), lambda i, j, s: (j, 0, 0, 0)),
                      pl.BlockSpec((1, ge, d), lambda i, j, s: (j, 0, 0))],
            out_specs=pl.BlockSpec((tm, d), lambda i, j, s: (i, 0)),
            scratch_shapes=[pltpu.VMEM((tm, d), F32)]),
        out_shape=jax.ShapeDtypeStruct((t, d), BF16),
        compiler_params=_cparams(("parallel", "arbitrary")),
        name="moe_experts",
    )(seg_flat, hs, cws, wg, wu, wd)
    return pl.pallas_call(
        functools.partial(_moe_unsort_kernel, final_norm=final_norm),
        grid=(nt,),
        in_specs=[row1(d), row1(d), row1(LANE), full1((1, d))],
        out_specs=row1(d),
        out_shape=jax.ShapeDtypeStruct((t, d), F32),
        compiler_params=_cparams(("parallel",)),
        name="moe_unsort",
    )(x2d, ys, pos, fnorm)


def _inproj_weight(w_in):
    d = w_in.shape[0]
    kr = w_in[:, _IN_M + Q_LORA + KV_LORA:_IN_H]
    half = QK_ROPE // 2
    kr_rot = jnp.concatenate([-kr[:, half:], kr[:, :half]], axis=1)
    z64 = jnp.zeros((d, QK_NOPE), F32)
    z32 = jnp.zeros((d, LANE - QK_NOPE - QK_ROPE), F32)
    cols = [w_in[:, _IN_G:], w_in[:, _IN_A:_IN_S], w_in[:, _IN_H:_IN_G], w_in[:, _IN_S:_IN_M],
            w_in[:, _IN_M:_IN_M + Q_LORA], w_in[:, _IN_M + Q_LORA:_IN_M + Q_LORA + KV_LORA],
            z64, kr, z32, z64, kr_rot, z32]
    return jnp.concatenate(cols, axis=1).astype(BF16)


def _rope_tables(positions):
    inv_freq = 1.0 / (ROPE_BASE ** (jnp.arange(0, QK_ROPE, 2, dtype=F32) / QK_ROPE))
    ang = positions.astype(F32)[..., None] * inv_freq
    cos, sin = jnp.cos(ang), jnp.sin(ang)
    shp = positions.shape
    pad = jnp.zeros(shp + (HEAD_PAD - QK_NOPE - QK_ROPE,), F32)
    cos_t = jnp.concatenate([jnp.ones(shp + (QK_NOPE,), F32), cos, cos, pad], axis=-1)
    sin_t = jnp.concatenate([jnp.zeros(shp + (QK_NOPE,), F32), sin, sin, pad], axis=-1)
    return cos_t, sin_t


def kernel(x, mem, positions, mix_norm, w_in, gate_bias, conv_a, w_out_a, s5_lambda_re, s5_lambda_im, s5_log_step, s5_b_re, s5_b_im, s5_c_re, s5_c_im, s5_d, s5_w_glu, mla_q_norm, mla_w_q_b, mla_kv_norm, mla_w_kv_b, mla_w_o, hy_conv, hy_f_w1, hy_f_b1, hy_f_w2, hy_f_b2, hy_f_w3, hy_f_freq, hy_bias, hy_w_out, w_mix_out, xa_norm, mem_norm, xa_w_q, xa_w_kv, xa_w_o, moe_norm, moe_w_group, moe_b_group, moe_w_expert, moe_b_expert, moe_w_gate, moe_w_up, moe_w_down, final_norm):
    b, l, d = x.shape
    depth = w_in.shape[0]
    cos_t, sin_t = _rope_tables(positions)
    tables = _hy_tables(l)
    for i in range(depth):
        p3 = _inproj(x.reshape(b * l, d), mix_norm[i][None], _inproj_weight(w_in[i])).reshape(b, l, N_P)
        a_pre = _shortconv(p3, conv_a[i])
        ys = _s5(p3, _s5_operators(s5_lambda_re[i], s5_lambda_im[i], s5_log_step[i], s5_b_re[i], s5_b_im[i],
                                   s5_c_re[i], s5_c_im[i], s5_d[i]))
        wq, wkv = _mla_weights(mla_w_q_b[i], mla_w_kv_b[i])
        q, k, v = _mla_prep(p3, cos_t, sin_t, mla_q_norm[i][None], mla_kv_norm[i][None], wq, wkv)
        o_mla = _mla_attn(q, k, v)
        kf_re, kf_im = _hy_filter_spectrum(l, hy_f_w1[i], hy_f_b1[i], hy_f_w2[i], hy_f_b2[i], hy_f_w3[i],
                                           hy_f_freq[i], tables)
        z_hy = _hy_conv(p3, hy_conv[i], kf_re, kf_im, hy_bias[i], tables)
        x = _merge(x, p3, a_pre, ys, o_mla, z_hy, gate_bias[i], w_out_a[i], s5_w_glu[i], mla_w_o[i],
                   hy_w_out[i], w_mix_out[i])
        kv = _mem_kv(mem, mem_norm[None], xa_w_kv[i])
        x = _xattn(x, kv, xa_norm[i][None], xa_w_q[i], xa_w_o[i])
        x = _moe(x.reshape(b * l, d), moe_norm[i][None], moe_w_group[i], moe_b_group[i], moe_w_expert[i],
                 moe_b_expert[i], moe_w_gate[i], moe_w_up[i], moe_w_down[i], final_norm[None],
                 final_norm=(i == depth - 1)).reshape(b, l, d)
    return x
```

```python
import functools
import math

import numpy as np
import jax
import jax.numpy as jnp
from jax import lax
from jax.experimental import pallas as pl
from jax.experimental.pallas import tpu as pltpu

F32 = jnp.float32
BF16 = jnp.bfloat16
EPS = 1e-6

D_MODEL = 1024
N_BRANCH = 4
W_A = 512
W_S = 512
S5_GROUP = 16
S5_GROUPS = W_S // S5_GROUP
S5_STATE = 64
S5_CHUNK = 16
MLA_HEADS = 8
Q_LORA = 256
KV_LORA = 256
QK_NOPE = 64
QK_ROPE = 32
V_DIM = 64
ROPE_BASE = 10000.0
W_H = 512
HY_ORDER = 2
HY_EMB = 33
HY_BANDS = (HY_EMB - 1) // 2
HY_FO = 64
HY_FAST_DECAY = 0.3
HY_SLOW_DECAY = 1.5
HY_TARGET = 1e-2
XA_HEADS = 4
XA_DH = 128
N_GROUPS = 4
EXP_PER_GROUP = 8
N_EXPERTS = N_GROUPS * EXP_PER_GROUP
D_FF_E = 256

LANE = 128
HEAD_PAD = 128
VMEM_LIMIT = 56 * 1024 * 1024

OFF_G = 0
OFF_A = OFF_G + N_BRANCH * D_MODEL
OFF_H = OFF_A + 3 * W_A
OFF_S = OFF_H + 3 * W_H
OFF_CQ = OFF_S + W_S
OFF_CKV = OFF_CQ + Q_LORA
OFF_KRA = OFF_CKV + KV_LORA
OFF_KRB = OFF_KRA + LANE
N_P = OFF_KRB + LANE

_IN_A = 0
_IN_S = 3 * W_A
_IN_M = _IN_S + W_S
_IN_H = _IN_M + Q_LORA + KV_LORA + QK_ROPE
_IN_G = _IN_H + 3 * W_H

HY_N1 = 8
HY_CT = 128
HY_NCT = W_H // HY_CT


def _cparams(sem, vmem=VMEM_LIMIT):
    return pltpu.CompilerParams(dimension_semantics=sem, vmem_limit_bytes=vmem)


def _split_bf16(x):
    hi = x.astype(BF16)
    lo = (x - hi.astype(F32)).astype(BF16)
    return hi, lo


def _dot(a, b):
    return jnp.dot(a, b, preferred_element_type=F32)


def _dot3(a_hi, a_lo, b_hi, b_lo):
    return _dot(a_hi, b_hi) + (_dot(a_lo, b_hi) + _dot(a_hi, b_lo))


def _rms(x, g):
    return x * lax.rsqrt(jnp.mean(x * x, axis=-1, keepdims=True) + EPS) * g


def _inproj_kernel(x_ref, g_ref, w_ref, o_ref, h_ref):
    @pl.when(pl.program_id(1) == 0)
    def _():
        h_ref[...] = _rms(x_ref[...], g_ref[...]).astype(BF16)

    o_ref[...] = _dot(h_ref[...], w_ref[...]).astype(o_ref.dtype)


def _inproj(x2d, g, w, *, tm=1024, tn=N_P // 3):
    t = x2d.shape[0]
    return pl.pallas_call(
        _inproj_kernel,
        grid=(t // tm, N_P // tn),
        in_specs=[pl.BlockSpec((tm, D_MODEL), lambda i, j: (i, 0)),
                  pl.BlockSpec((1, D_MODEL), lambda i, j: (0, 0)),
                  pl.BlockSpec((D_MODEL, tn), lambda i, j: (0, j))],
        out_specs=pl.BlockSpec((tm, tn), lambda i, j: (i, j)),
        out_shape=jax.ShapeDtypeStruct((t, N_P), BF16),
        scratch_shapes=[pltpu.VMEM((tm, D_MODEL), BF16)],
        compiler_params=_cparams(("parallel", "arbitrary")),
        name="inproj",
    )(x2d, g, w)


def _conv3(u, w):
    n = u.shape[0]
    row = lax.broadcasted_iota(jnp.int32, u.shape, 0)
    prev = jnp.where(row == 0, 0.0, pltpu.roll(u, 1, axis=0))
    nxt = jnp.where(row == n - 1, 0.0, pltpu.roll(u, n - 1, axis=0))
    return w[0:1] * prev + w[1:2] * u + w[2:3] * nxt


def _shortconv_kernel(bg_ref, cg_ref, xi_ref, wa_ref, a_ref):
    f32 = lambda r: r[0].astype(F32)
    a_ref[0] = f32(bg_ref) * _conv3(f32(cg_ref) * f32(xi_ref), wa_ref[0])


def _shortconv(p3, conv_a):
    b, l, _ = p3.shape
    nct = W_A // LANE
    wa = conv_a.reshape(3, nct, LANE).transpose(1, 0, 2)

    def pspec(off):
        return pl.BlockSpec((1, l, LANE), lambda i, j, off=off: (i, 0, off // LANE + j))

    return pl.pallas_call(
        _shortconv_kernel,
        grid=(b, nct),
        in_specs=[pspec(OFF_A), pspec(OFF_A + W_A), pspec(OFF_A + 2 * W_A),
                  pl.BlockSpec((1, 3, LANE), lambda i, j: (j, 0, 0))],
        out_specs=pl.BlockSpec((1, l, LANE), lambda i, j: (i, 0, j)),
        out_shape=jax.ShapeDtypeStruct((b, l, W_A), F32),
        compiler_params=_cparams(("parallel", "parallel")),
        name="shortconv",
    )(p3, p3, p3, wa)


S5_GPB = LANE // S5_GROUP
S5_NLB = W_S // LANE
S5_XW = S5_CHUNK * LANE
S5_SW = S5_GPB * 2 * S5_STATE


def _s5_operators(lam_re, lam_im, log_step, b_re, b_im, c_re, c_im, d_skip):
    q, hh, g = S5_CHUNK, S5_GROUP, S5_GROUPS
    hp = lax.Precision.HIGHEST
    delta = jnp.exp(log_step)[..., None]

    def powers(exps):
        e = jnp.asarray(exps, F32)
        mag = jnp.exp((lam_re * delta)[..., None] * e)
        ang = (lam_im * delta)[..., None] * e
        return mag * jnp.cos(ang), mag * jnp.sin(ang)

    ramp = np.arange(q)
    p1r, p1i = powers([1.0])
    den = lam_re * lam_re + lam_im * lam_im
    nr, ni = p1r[..., 0] - 1.0, p1i[..., 0]
    fr = (nr * lam_re + ni * lam_im) / den
    fi = (ni * lam_re - nr * lam_im) / den
    bbr = fr[..., None] * b_re - fi[..., None] * b_im
    bbi = fr[..., None] * b_im + fi[..., None] * b_re

    def times_bbar(exps):
        p_r, p_i = powers(exps)
        return (p_r[..., None] * bbr[..., None, :] - p_i[..., None] * bbi[..., None, :],
                p_r[..., None] * bbi[..., None, :] + p_i[..., None] * bbr[..., None, :])

    mr, mi = times_bbar(ramp)
    kern = (jnp.einsum('dghp,dgpek->dgehk', c_re, mr, precision=hp)
            - jnp.einsum('dghp,dgpek->dgehk', c_im, mi, precision=hp))
    i_idx = jnp.arange(q)[:, None]
    j_idx = jnp.arange(q)[None, :]
    diff_f = jnp.clip(j_idx - i_idx, 0, q - 1)
    diff_b = jnp.clip(i_idx - j_idx, 0, q - 1)
    t_f = jnp.where((i_idx <= j_idx)[None, :, :, None, None], kern[0][:, diff_f], 0.0)
    t_b = jnp.where((i_idx >= j_idx)[None, :, :, None, None], kern[1][:, diff_b], 0.0)
    t_all = (t_f + t_b).transpose(0, 1, 4, 2, 3)
    skip = jnp.eye(q, dtype=F32)[:, None, :, None] * jnp.eye(hh, dtype=F32)[None, :, None, :]
    t_all = t_all + skip[None] * d_skip.reshape(g, 1, 1, 1, hh)
    eye_g = jnp.eye(S5_GPB, dtype=BF16)
    blk = lambda a: a.astype(BF16).reshape((S5_NLB, S5_GPB) + a.shape[1:])
    r_idx = lax.broadcasted_iota(jnp.int32, (S5_GPB, q * hh, S5_XW), 1)
    n_idx = lax.broadcasted_iota(jnp.int32, (S5_GPB, q * hh, S5_XW), 2)
    g_idx = lax.broadcasted_iota(jnp.int32, (S5_GPB, q * hh, S5_XW), 0)
    spread = (n_idx == (r_idx // hh) * LANE + g_idx * hh + r_idx % hh).astype(BF16)
    cols_big = lambda a: jnp.einsum('lgrc,gcn->lgrn', blk(a), spread, preferred_element_type=BF16)
    t_big = cols_big(t_all.reshape(g, q * hh, q * hh)).reshape(S5_NLB, S5_GPB, q, hh, S5_XW)
    t_big = t_big.transpose(0, 2, 1, 3, 4).reshape(S5_NLB, S5_XW, S5_XW)

    er, ei = times_bbar(q - 1 - ramp)
    state_in = lambda f, b: jnp.concatenate([f[0].transpose(0, 2, 3, 1), b[1].transpose(0, 2, 3, 1)], axis=-1)

    def st_big(a):
        a5 = blk(a).transpose(0, 2, 1, 3, 4)
        return (a5[:, :, :, :, None, :] * eye_g[None, None, :, None, :, None]).reshape(S5_NLB, S5_XW, S5_SW)

    w1 = jnp.concatenate([t_big, st_big(state_in(er, mr)), st_big(state_in(ei, mi))], axis=-1)

    def coef(c_r, c_i, p_r, p_i):
        cr, ci = c_r.transpose(0, 2, 1)[:, :, None, :], c_i.transpose(0, 2, 1)[:, :, None, :]
        return cr * p_r[..., None] - ci * p_i[..., None], -(cr * p_i[..., None] + ci * p_r[..., None])

    pf_r, pf_i = powers(ramp + 1)
    pb_r, pb_i = powers(q - ramp)
    f_re, f_im = coef(c_re[0], c_im[0], pf_r[0], pf_i[0])
    r_re, r_im = coef(c_re[1], c_im[1], pb_r[1], pb_i[1])
    out_big = lambda a: cols_big(a.reshape(g, 2 * S5_STATE, q * hh)).reshape(S5_NLB, S5_SW, S5_XW)
    wout = jnp.concatenate([out_big(jnp.concatenate([f_re, r_re], axis=1)),
                            out_big(jnp.concatenate([f_im, r_im], axis=1))], axis=1)

    lanes = lambda a: jnp.concatenate([a[0], a[1]], axis=-1).reshape(S5_NLB, S5_SW)
    pq_r, pq_i = powers([float(q)])
    lam = jnp.stack([lanes(pq_r[..., 0]), lanes(pq_i[..., 0])], axis=1)
    return w1, wout, lam


def _s5_kernel(u_ref, w1_ref, wo_ref, lam_ref, y_ref, u32_ref, yin_ref, sre_ref, sim_ref, xre_ref, xim_ref,
               *, n_chunks):
    q, gpb, half = S5_CHUNK, S5_GPB, S5_STATE
    u32_ref[...] = u_ref[0].astype(F32)
    xcat = jnp.concatenate([u32_ref[pl.ds(i, n_chunks, stride=q), :].astype(BF16) for i in range(q)], axis=-1)
    m1 = _dot(xcat, w1_ref[0])
    yin_ref[...] = m1[:, :S5_XW]
    for g in range(gpb):
        sre_ref[pl.ds(g, n_chunks, stride=gpb), :] = m1[:, S5_XW + g * LANE:S5_XW + (g + 1) * LANE]
        sim_ref[pl.ds(g, n_chunks, stride=gpb), :] = m1[:, S5_XW + S5_SW + g * LANE:S5_XW + S5_SW + (g + 1) * LANE]
    lr = lam_ref[0, 0]
    li = lam_ref[0, 1]
    fwd_lane = lax.broadcasted_iota(jnp.int32, (gpb, LANE), 1) < half

    def step(k, carry):
        xr, xi = carry
        rf = pl.multiple_of(k * gpb, gpb)
        rb = pl.multiple_of((n_chunks - 1 - k) * gpb, gpb)
        xre_ref[pl.ds(rf, gpb), 0:half] = xr[:, 0:half]
        xim_ref[pl.ds(rf, gpb), 0:half] = xi[:, 0:half]
        xre_ref[pl.ds(rb, gpb), half:LANE] = xr[:, half:LANE]
        xim_ref[pl.ds(rb, gpb), half:LANE] = xi[:, half:LANE]
        ar = jnp.where(fwd_lane, sre_ref[pl.ds(rf, gpb), :], sre_ref[pl.ds(rb, gpb), :])
        ai = jnp.where(fwd_lane, sim_ref[pl.ds(rf, gpb), :], sim_ref[pl.ds(rb, gpb), :])
        return lr * xr - li * xi + ar, lr * xi + li * xr + ai

    zero = jnp.zeros((gpb, LANE), F32)
    lax.fori_loop(0, n_chunks, step, (zero, zero))
    xs = jnp.concatenate([r[pl.ds(g, n_chunks, stride=gpb), :].astype(BF16)
                          for r in (xre_ref, xim_ref) for g in range(gpb)], axis=-1)
    y = yin_ref[...] + _dot(xs, wo_ref[0])
    for j in range(q):
        y_ref[0, pl.ds(j, n_chunks, stride=q), :] = y[:, j * LANE:(j + 1) * LANE]


def _s5(p3, ops):
    b, l, _ = p3.shape
    nc = l // S5_CHUNK
    w1, wout, lam = ops
    lam = lam.reshape(S5_NLB, 2, S5_GPB, LANE)
    once = dict(pipeline_mode=pl.Buffered(1))
    return pl.pallas_call(
        functools.partial(_s5_kernel, n_chunks=nc),
        grid=(S5_NLB, b),
        in_specs=[pl.BlockSpec((1, l, LANE), lambda j, i: (i, 0, OFF_S // LANE + j)),
                  pl.BlockSpec((1, S5_XW, S5_XW + 2 * S5_SW), lambda j, i: (j, 0, 0), **once),
                  pl.BlockSpec((1, 2 * S5_SW, S5_XW), lambda j, i: (j, 0, 0), **once),
                  pl.BlockSpec((1, 2, S5_GPB, LANE), lambda j, i: (j, 0, 0, 0))],
        out_specs=pl.BlockSpec((1, l, LANE), lambda j, i: (i, 0, j)),
        out_shape=jax.ShapeDtypeStruct((b, l, W_S), F32),
        scratch_shapes=[pltpu.VMEM((l, LANE), F32), pltpu.VMEM((nc, S5_XW), F32)]
        + [pltpu.VMEM((nc * S5_GPB, LANE), F32)] * 4,
        compiler_params=_cparams(("parallel", "arbitrary")),
        name="s5_scan",
    )(p3, w1, wout, lam)


def _mla_prep_kernel(cq_ref, ckv_ref, kra_ref, krb_ref, cos_ref, sin_ref, qn_ref, kvn_ref,
                     wq_ref, wkv_ref, q_ref, k_ref, v_ref):
    cqn = _rms(cq_ref[0].astype(F32), qn_ref[...]).astype(BF16)
    ckvn = _rms(ckv_ref[0].astype(F32), kvn_ref[...]).astype(BF16)
    cos = cos_ref[0]
    sin = sin_ref[0]
    kr = kra_ref[0].astype(F32) * cos + krb_ref[0].astype(F32) * sin
    lane = lax.broadcasted_iota(jnp.int32, cos.shape, 1)
    ones_col = jnp.where(lane == V_DIM, 1.0, 0.0)
    scale = (QK_NOPE + QK_ROPE) ** -0.5
    for h in range(MLA_HEADS):
        qq = _dot(cqn, wq_ref[h])
        q_ref[0, h] = ((qq[:, :HEAD_PAD] * cos + qq[:, HEAD_PAD:] * sin) * scale).astype(BF16)
        kv = _dot(ckvn, wkv_ref[h])
        k_ref[0, h] = (kv[:, :HEAD_PAD] + kr).astype(BF16)
        v_ref[0, h] = (kv[:, HEAD_PAD:] + ones_col).astype(BF16)


def _mla_weights(w_q_b, w_kv_b):
    dq = QK_NOPE + QK_ROPE
    half = QK_ROPE // 2
    wq = w_q_b.reshape(Q_LORA, MLA_HEADS, dq).transpose(1, 0, 2)
    rope = wq[..., QK_NOPE:]
    rot = jnp.concatenate([-rope[..., half:], rope[..., :half]], axis=-1)
    zq = jnp.zeros((MLA_HEADS, Q_LORA, HEAD_PAD - dq), F32)
    zn = jnp.zeros((MLA_HEADS, Q_LORA, QK_NOPE), F32)
    wq_full = jnp.concatenate([wq, zq, zn, rot, zq], axis=-1)
    wkv = w_kv_b.reshape(KV_LORA, MLA_HEADS, QK_NOPE + V_DIM).transpose(1, 0, 2)
    zk = jnp.zeros((MLA_HEADS, KV_LORA, HEAD_PAD - QK_NOPE), F32)
    zv = jnp.zeros((MLA_HEADS, KV_LORA, HEAD_PAD - V_DIM), F32)
    wkv_full = jnp.concatenate([wkv[..., :QK_NOPE], zk, wkv[..., QK_NOPE:], zv], axis=-1)
    return wq_full.astype(BF16), wkv_full.astype(BF16)


def _mla_prep(p3, cos_t, sin_t, q_norm, kv_norm, wq, wkv, *, tl=512):
    b, l, _ = p3.shape
    hspec = pl.BlockSpec((1, MLA_HEADS, tl, HEAD_PAD), lambda i, j: (i, 0, j, 0))
    hshape = jax.ShapeDtypeStruct((b, MLA_HEADS, l, HEAD_PAD), BF16)
    tab = pl.BlockSpec((1, tl, LANE), lambda i, j: (i, j, 0))
    return pl.pallas_call(
        _mla_prep_kernel,
        grid=(b, l // tl),
        in_specs=[pl.BlockSpec((1, tl, Q_LORA), lambda i, j: (i, j, OFF_CQ // Q_LORA)),
                  pl.BlockSpec((1, tl, KV_LORA), lambda i, j: (i, j, OFF_CKV // KV_LORA)),
                  pl.BlockSpec((1, tl, LANE), lambda i, j: (i, j, OFF_KRA // LANE)),
                  pl.BlockSpec((1, tl, LANE), lambda i, j: (i, j, OFF_KRB // LANE)),
                  tab, tab,
                  pl.BlockSpec((1, Q_LORA), lambda i, j: (0, 0)),
                  pl.BlockSpec((1, KV_LORA), lambda i, j: (0, 0)),
                  pl.BlockSpec((MLA_HEADS, Q_LORA, 2 * HEAD_PAD), lambda i, j: (0, 0, 0)),
                  pl.BlockSpec((MLA_HEADS, KV_LORA, 2 * HEAD_PAD), lambda i, j: (0, 0, 0))],
        out_specs=[hspec, hspec, hspec],
        out_shape=[hshape, hshape, hshape],
        compiler_params=_cparams(("parallel", "parallel")),
        name="mla_prep",
    )(p3, p3, p3, p3, cos_t, sin_t, q_norm, kv_norm, wq, wkv)


def _mla_attn_kernel(q_ref, k_ref, v_ref, o_ref, *, tq):
    def q_tile(t, carry):
        rows = pl.ds(pl.multiple_of(t * tq, tq), tq)
        outs = []
        for h in range(2):
            s = lax.dot_general(q_ref[0, h, rows, :], k_ref[0, h], (((1,), (1,)), ((), ())),
                                preferred_element_type=F32)
            m = jnp.max(s, axis=-1, keepdims=True)
            p = jnp.exp(s - m).astype(BF16)
            o = _dot(p, v_ref[0, h])
            outs.append(o / o[:, V_DIM:V_DIM + 1])
        lane = lax.broadcasted_iota(jnp.int32, outs[0].shape, 1)
        o_ref[0, rows, :] = jnp.where(lane < V_DIM, outs[0], pltpu.roll(outs[1], V_DIM, axis=1))
        return carry

    lax.fori_loop(0, q_ref.shape[2] // tq, q_tile, 0, unroll=2)


def _mla_attn(q, k, v, *, tq=256):
    b, _, l, _ = q.shape
    hspec = pl.BlockSpec((1, 2, l, HEAD_PAD), lambda i, j: (i, j, 0, 0))
    return pl.pallas_call(
        functools.partial(_mla_attn_kernel, tq=tq),
        grid=(b, MLA_HEADS // 2),
        in_specs=[hspec, hspec, hspec],
        out_specs=pl.BlockSpec((1, l, 2 * V_DIM), lambda i, j: (i, 0, j)),
        out_shape=jax.ShapeDtypeStruct((b, l, MLA_HEADS * V_DIM), F32),
        compiler_params=_cparams(("parallel", "parallel")),
        name="mla_attn",
    )(q, k, v)


def _hy_sizes(l):
    n = 2 * l
    n2 = n // HY_N1
    nf = n2 // 2 + 1
    nfp = ((nf + 63) // 64) * 64
    return n, n2, nf, nfp


def _hy_tables(l):
    n, n2, nf, nfp = _hy_sizes(l)
    f2 = np.arange(nfp)[:, None].astype(np.float64)
    t2 = np.arange(n2 // 2)[None, :].astype(np.float64)
    valid = (np.arange(nfp) < nf)[:, None]
    ang = 2.0 * np.pi * f2 * t2 / n2
    fwd = np.concatenate([np.where(valid, np.cos(ang), 0.0), np.where(valid, -np.sin(ang), 0.0)], axis=0)
    wgt = np.where((np.arange(nfp) == 0) | (np.arange(nfp) == nf - 1), 1.0, 2.0)[:, None] * valid / n
    inv = np.concatenate([(wgt * np.cos(ang)).T, (-wgt * np.sin(ang)).T], axis=1)
    t1 = np.arange(HY_N1)[None, :].astype(np.float64)
    tw_ang = 2.0 * np.pi * f2 * t1 / n
    tw_re = np.repeat(np.cos(tw_ang), HY_CT, axis=1).astype(np.float32)
    tw_im = np.repeat(-np.sin(tw_ang), HY_CT, axis=1).astype(np.float32)
    fwd_hi, fwd_lo = _split_bf16(jnp.asarray(fwd, F32))
    return fwd_hi, fwd_lo, jnp.asarray(inv, F32).astype(BF16), jnp.asarray(tw_re), jnp.asarray(tw_im)


def _cmul(a, b):
    return a[0] * b[0] - a[1] * b[1], a[0] * b[1] + a[1] * b[0]


def _cadd(a, b):
    return a[0] + b[0], a[1] + b[1]


def _csub(a, b):
    return a[0] - b[0], a[1] - b[1]


def _cmul_i(a, sign):
    return (-a[1], a[0]) if sign > 0 else (a[1], -a[0])


def _fft4(a, sign):
    s0, s1 = _cadd(a[0], a[2]), _csub(a[0], a[2])
    s2, s3 = _cadd(a[1], a[3]), _csub(a[1], a[3])
    r3 = _cmul_i(s3, sign)
    return [_cadd(s0, s2), _cadd(s1, r3), _csub(s0, s2), _csub(s1, r3)]


def _fft8(x, sign):
    e = _fft4([x[0], x[2], x[4], x[6]], sign)
    o = _fft4([x[1], x[3], x[5], x[7]], sign)
    r = math.sqrt(0.5)
    o1 = ((o[1][0] - sign * o[1][1]) * r, (o[1][1] + sign * o[1][0]) * r)
    o2 = _cmul_i(o[2], sign)
    o3 = ((-o[3][0] - sign * o[3][1]) * r, (-o[3][1] + sign * o[3][0]) * r)
    tw = [o[0], o1, o2, o3]
    return [_cadd(e[k], tw[k]) for k in range(4)] + [_csub(e[k], tw[k]) for k in range(4)]


def _blocks(ref_re, ref_im, rows):
    return [(ref_re[rows, k * HY_CT:(k + 1) * HY_CT], ref_im[rows, k * HY_CT:(k + 1) * HY_CT])
            for k in range(HY_N1)]


def _hy_spectrum(z_ref, twr_ref, twi_ref, rows, nfp):
    t = []
    for k in range(HY_N1):
        sl = slice(k * HY_CT, (k + 1) * HY_CT)
        zk = (z_ref[rows, sl], z_ref[pl.ds(pl.multiple_of(nfp + rows.start, rows.size), rows.size), sl])
        t.append(_cmul(zk, (twr_ref[rows, sl], twi_ref[rows, sl])))
    return _fft8(t, -1)


def _hy_fold(nat_ref, rows):
    return jnp.concatenate([nat_ref[pl.ds(t1, rows, stride=HY_N1), :] for t1 in range(HY_N1)], axis=-1)


def _hy_filter_kernel(z_ref, w1_ref, b1_ref, w2_ref, b2_ref, w3_ref, fr_ref, tn_ref, dl_ref,
                      fh_ref, fl_ref, twr_ref, twi_ref, kr_ref, ki_ref, ff_ref, fb_ref, zf_ref, zb_ref,
                      *, nfp, row_chunk, rows):
    hp = lax.Precision.HIGHEST
    fr = fr_ref[...]
    h = jnp.sin(fr * (jnp.dot(z_ref[...], w1_ref[...], precision=hp, preferred_element_type=F32) + b1_ref[...]))
    h = jnp.sin(fr * (jnp.dot(h, w2_ref[...], precision=hp, preferred_element_type=F32) + b2_ref[...]))
    filt = jnp.dot(h, w3_ref[0, 0], precision=hp, preferred_element_type=F32)
    decay = jnp.exp(-tn_ref[...] * dl_ref[0])
    fwd = filt[:, :HY_CT] * decay
    row = lax.broadcasted_iota(jnp.int32, fwd.shape, 0)
    bwd = jnp.where(row == 0, 0.0, filt[:, HY_CT:] * decay)
    inv = lax.rsqrt(jnp.sum(fwd * fwd, axis=0, keepdims=True) + jnp.sum(bwd * bwd, axis=0, keepdims=True) + EPS)
    ff_ref[...] = fwd
    fb_ref[...] = bwd
    for src, dst in ((ff_ref, zf_ref), (fb_ref, zb_ref)):
        k_hi, k_lo = _split_bf16(_hy_fold(src, rows))
        dst[...] = _dot3(fh_ref[...], fl_ref[...], k_hi, k_lo)
    def chunk(c, carry):
        rws = pl.ds(pl.multiple_of(c * row_chunk, row_chunk), row_chunk)
        sf = _hy_spectrum(zf_ref, twr_ref, twi_ref, rws, nfp)
        sb = _hy_spectrum(zb_ref, twr_ref, twi_ref, rws, nfp)
        for f1 in range(HY_N1):
            sl = slice(f1 * HY_CT, (f1 + 1) * HY_CT)
            kr_ref[0, 0, rws, sl] = (sf[f1][0] + sb[f1][0]) * inv
            ki_ref[0, 0, rws, sl] = (sf[f1][1] - sb[f1][1]) * inv
        return carry

    lax.fori_loop(0, nfp // row_chunk, chunk, 0)


HY_ROW_CHUNK = 16


def _hy_row_chunk(nfp):
    assert nfp % HY_ROW_CHUNK == 0
    return HY_ROW_CHUNK


def _hy_conv_chunk(nfp):
    return next(c for c in (96, 64, 48, 32, 16) if nfp % c == 0)


def _hy_filter_spectrum(l, w1, b1, w2, b2, w3, freq, tables):
    n, n2, nf, nfp = _hy_sizes(l)
    fwd_hi, fwd_lo, _, tw_re, tw_im = tables
    rows = l // HY_N1
    wide = HY_N1 * HY_CT
    t = np.arange(l, dtype=np.float32)
    t_norm = t / np.float32(max(l - 1, 1))
    bands = np.linspace(1e-4, HY_BANDS - 1, HY_BANDS, dtype=np.float32)
    ang = np.float32(2.0 * math.pi / l) * t[:, None] * bands[None]
    z = np.concatenate([t_norm[:, None], np.cos(ang), -np.sin(ang)], axis=-1).astype(np.float32)
    kpad = 40
    z = np.pad(z, ((0, 0), (0, kpad - HY_EMB)))
    w1p = jnp.pad(w1, ((0, kpad - HY_EMB), (0, 0)))
    max_decay = math.log(HY_TARGET) / HY_FAST_DECAY
    min_decay = math.log(HY_TARGET) / HY_SLOW_DECAY
    deltas = np.abs(np.linspace(min_decay, max_decay, W_H, dtype=np.float32)).reshape(HY_NCT, 1, HY_CT)
    w3t = w3.reshape(HY_FO, HY_ORDER, 2, HY_NCT, HY_CT).transpose(1, 3, 0, 2, 4).reshape(HY_ORDER, HY_NCT, HY_FO, 2 * HY_CT)
    full = lambda s: pl.BlockSpec(s, lambda i, j: (0, 0))
    ospec = pl.BlockSpec((1, 1, nfp, wide), lambda i, j: (i, j, 0, 0))
    oshape = jax.ShapeDtypeStruct((HY_ORDER, HY_NCT, nfp, wide), F32)
    return pl.pallas_call(
        functools.partial(_hy_filter_kernel, nfp=nfp, row_chunk=_hy_row_chunk(nfp), rows=rows),
        grid=(HY_ORDER, HY_NCT),
        in_specs=[full((l, kpad)), full((kpad, HY_FO)), full((1, HY_FO)), full((HY_FO, HY_FO)), full((1, HY_FO)),
                  pl.BlockSpec((1, 1, HY_FO, 2 * HY_CT), lambda i, j: (i, j, 0, 0)),
                  full((1, HY_FO)), full((l, 1)),
                  pl.BlockSpec((1, 1, HY_CT), lambda i, j: (j, 0, 0)),
                  full((2 * nfp, rows)), full((2 * nfp, rows)), full((nfp, wide)), full((nfp, wide))],
        out_specs=[ospec, ospec],
        out_shape=[oshape, oshape],
        scratch_shapes=[pltpu.VMEM((l, HY_CT), F32), pltpu.VMEM((l, HY_CT), F32),
                        pltpu.VMEM((2 * nfp, wide), F32), pltpu.VMEM((2 * nfp, wide), F32)],
        compiler_params=_cparams(("parallel", "parallel")),
        name="hyena_filter",
    )(jnp.asarray(z), w1p, b1[None], w2, b2[None], w3t, freq[None], jnp.asarray(t_norm[:, None]),
      jnp.asarray(deltas), fwd_hi, fwd_lo, tw_re, tw_im)


def _hy_conv_kernel(pv_ref, p1_ref, p2_ref, wc_ref, kr_ref, ki_ref, bias_ref, fwd_ref, inv_ref, twr_ref, twi_ref,
                    o_ref, nat_ref, z_ref, u_ref, *, nfp, row_chunk, rows):
    def folded_conv3(p_ref, part):
        nat_ref[...] = _conv3(p_ref[0].astype(F32), wc_ref[0, part])
        return _hy_fold(nat_ref, rows)

    def long_conv(u, order):
        z_ref[...] = _dot(fwd_ref[...], u.astype(BF16))
        for c in range(nfp // row_chunk):
            rws = pl.ds(c * row_chunk, row_chunk)
            spec = _hy_spectrum(z_ref, twr_ref, twi_ref, rws, nfp)
            kf = _blocks(kr_ref.at[order, 0], ki_ref.at[order, 0], rws)
            y = _fft8([_cmul(spec[f1], kf[f1]) for f1 in range(HY_N1)], +1)
            for t1 in range(HY_N1):
                sl = slice(t1 * HY_CT, (t1 + 1) * HY_CT)
                w = _cmul(y[t1], (twr_ref[rws, sl], -twi_ref[rws, sl]))
                u_ref[rws, sl] = w[0].astype(BF16)
                u_ref[pl.ds(nfp + c * row_chunk, row_chunk), sl] = w[1].astype(BF16)
        return _dot(inv_ref[...], u_ref[...])

    v = folded_conv3(pv_ref, 0)
    x1 = folded_conv3(p1_ref, 1)
    x2 = folded_conv3(p2_ref, 2)
    bias = bias_ref[0]
    z1 = x1 * (long_conv(v, 0) + v * bias[0:1])
    z2 = x2 * (long_conv(z1, 1) + z1 * bias[1:2])
    for t1 in range(HY_N1):
        o_ref[0, pl.ds(t1, rows, stride=HY_N1), :] = z2[:, t1 * HY_CT:(t1 + 1) * HY_CT]


def _hy_conv(p3, hy_conv, kf_re, kf_im, bias, tables):
    b, l, _ = p3.shape
    n, n2, nf, nfp = _hy_sizes(l)
    fwd_hi, _, inv_t, tw_re, tw_im = tables
    rows = l // HY_N1
    wide = HY_N1 * HY_CT
    wc = hy_conv.reshape(3, 3, HY_NCT, HY_CT).transpose(2, 1, 0, 3)
    bias_t = jnp.tile(bias.reshape(HY_ORDER, HY_NCT, 1, HY_CT), (1, 1, HY_N1, 1))
    bias_t = bias_t.transpose(1, 0, 2, 3).reshape(HY_NCT, HY_ORDER, wide)
    once = dict(pipeline_mode=pl.Buffered(1))

    def pspec(part):
        return pl.BlockSpec((1, l, HY_CT), lambda j, i, part=part: (i, 0, (OFF_H + part * W_H) // HY_CT + j))

    kspec = pl.BlockSpec((HY_ORDER, 1, nfp, wide), lambda j, i: (0, j, 0, 0), **once)
    full = lambda s: pl.BlockSpec(s, lambda j, i: (0, 0), **once)
    return pl.pallas_call(
        functools.partial(_hy_conv_kernel, nfp=nfp, row_chunk=_hy_conv_chunk(nfp), rows=rows),
        grid=(HY_NCT, b),
        in_specs=[pspec(0), pspec(1), pspec(2),
                  pl.BlockSpec((1, 3, 3, HY_CT), lambda j, i: (j, 0, 0, 0)),
                  kspec, kspec,
                  pl.BlockSpec((1, HY_ORDER, wide), lambda j, i: (j, 0, 0)),
                  full((2 * nfp, rows)), full((rows, 2 * nfp)), full((nfp, wide)), full((nfp, wide))],
        out_specs=pl.BlockSpec((1, l, HY_CT), lambda j, i: (i, 0, j)),
        out_shape=jax.ShapeDtypeStruct((b, l, W_H), F32),
        scratch_shapes=[pltpu.VMEM((l, HY_CT), F32), pltpu.VMEM((2 * nfp, wide), F32),
                        pltpu.VMEM((2 * nfp, wide), BF16)],
        compiler_params=_cparams(("parallel", "arbitrary")),
        name="hyena_conv",
    )(p3, p3, p3, wc, kf_re, kf_im, bias_t, fwd_hi, inv_t, tw_re, tw_im)


def _merge_kernel(x_ref, pg_ref, a_ref, ys_ref, om_ref, zh_ref, gb_ref, wa_ref, wglu_ref, wo_ref,
                  wh_ref, wmix_ref, o_ref):
    d = D_MODEL
    y_a = _dot(a_ref[0].astype(BF16), wa_ref[...])
    glu = _dot(jax.nn.gelu(ys_ref[0]).astype(BF16), wglu_ref[...])
    y_s = glu[:, :d] * jax.nn.sigmoid(glu[:, d:])
    y_m = _dot(om_ref[0].astype(BF16), wo_ref[...])
    y_h = _dot(zh_ref[0].astype(BF16), wh_ref[...])
    gb = gb_ref[...]
    merged = jnp.zeros_like(y_a)
    for i, y in enumerate((y_a, y_s, y_m, y_h)):
        merged = merged + jax.nn.sigmoid(pg_ref[0, :, i * d:(i + 1) * d].astype(F32) + gb[i:i + 1]) * y
    o_ref[0] = x_ref[0] + _dot(merged.astype(BF16), wmix_ref[...])


def _merge(x, p3, a_pre, ys, o_mla, z_hy, gate_bias, w_out_a, w_glu, w_o, hy_w_out, w_mix, *, tm=512):
    b, l, d = x.shape
    row = lambda w: pl.BlockSpec((1, tm, w), lambda i, j: (i, j, 0))
    full = lambda s: pl.BlockSpec(s, lambda i, j: tuple(0 for _ in s))
    return pl.pallas_call(
        _merge_kernel,
        grid=(b, l // tm),
        in_specs=[row(d), row(N_BRANCH * d), row(W_A), row(W_S), row(MLA_HEADS * V_DIM), row(W_H),
                  full((N_BRANCH, d)), full((W_A, d)), full((W_S, 2 * d)), full((MLA_HEADS * V_DIM, d)),
                  full((W_H, d)), full((d, d))],
        out_specs=row(d),
        out_shape=jax.ShapeDtypeStruct((b, l, d), F32),
        compiler_params=_cparams(("parallel", "parallel")),
        name="merge",
    )(x, p3, a_pre, ys, o_mla, z_hy, gate_bias, w_out_a.astype(BF16), w_glu.astype(BF16),
      w_o.astype(BF16), hy_w_out.astype(BF16), w_mix.astype(BF16))


def _mem_kv_kernel(m_ref, g_ref, w_ref, o_ref):
    o_ref[0] = _dot(_rms(m_ref[0], g_ref[...]).astype(BF16), w_ref[...]).astype(BF16)


def _mem_kv(mem, mem_norm, w_kv):
    b, m, d = mem.shape
    n = w_kv.shape[1]
    return pl.pallas_call(
        _mem_kv_kernel,
        grid=(b,),
        in_specs=[pl.BlockSpec((1, m, d), lambda i: (i, 0, 0)),
                  pl.BlockSpec((1, d), lambda i: (0, 0)),
                  pl.BlockSpec((d, n), lambda i: (0, 0))],
        out_specs=pl.BlockSpec((1, m, n), lambda i: (i, 0, 0)),
        out_shape=jax.ShapeDtypeStruct((b, m, n), BF16),
        compiler_params=_cparams(("parallel",)),
        name="mem_kv",
    )(mem, mem_norm, w_kv.astype(BF16))


def _xattn_kernel(x_ref, g_ref, kv_ref, wq_ref, wo_ref, o_ref):
    x = x_ref[0]
    h = _rms(x, g_ref[...]).astype(BF16)
    q = (_dot(h, wq_ref[...]) * (XA_DH ** -0.5)).astype(BF16)
    outs = []
    for hd in range(XA_HEADS):
        k = kv_ref[0, :, hd * 2 * XA_DH:hd * 2 * XA_DH + XA_DH]
        v = kv_ref[0, :, hd * 2 * XA_DH + XA_DH:(hd + 1) * 2 * XA_DH]
        s = lax.dot_general(q[:, hd * XA_DH:(hd + 1) * XA_DH], k, (((1,), (1,)), ((), ())),
                            preferred_element_type=F32)
        e = jnp.exp(s - jnp.max(s, axis=-1, keepdims=True))
        p = (e / jnp.sum(e, axis=-1, keepdims=True)).astype(BF16)
        outs.append(_dot(p, v))
    o = jnp.concatenate(outs, axis=-1).astype(BF16)
    o_ref[0] = x + _dot(o, wo_ref[...])


def _xattn(x, kv, xa_norm, w_q, w_o, *, tm=512):
    b, l, d = x.shape
    m, n = kv.shape[1], kv.shape[2]
    full = lambda s: pl.BlockSpec(s, lambda i, j: tuple(0 for _ in s))
    return pl.pallas_call(
        _xattn_kernel,
        grid=(b, l // tm),
        in_specs=[pl.BlockSpec((1, tm, d), lambda i, j: (i, j, 0)), full((1, d)),
                  pl.BlockSpec((1, m, n), lambda i, j: (i, 0, 0)),
                  full((d, XA_HEADS * XA_DH)), full((XA_HEADS * XA_DH, d))],
        out_specs=pl.BlockSpec((1, tm, d), lambda i, j: (i, j, 0)),
        out_shape=jax.ShapeDtypeStruct((b, l, d), F32),
        compiler_params=_cparams(("parallel", "parallel")),
        name="xattn",
    )(x, xa_norm, kv, w_q.astype(BF16), w_o.astype(BF16))


def _moe_route(logits):
    neg = -jnp.inf
    big = float(1 << 20)
    lane = lax.broadcasted_iota(jnp.int32, logits.shape, 1).astype(F32)
    gl = jnp.where(lane < N_GROUPS, logits, neg)
    gmax = jnp.max(gl, axis=-1, keepdims=True)
    g_idx = jnp.min(jnp.where(gl == gmax, lane, big), axis=-1, keepdims=True)
    g_w = 1.0 / jnp.sum(jnp.exp(gl - gmax), axis=-1, keepdims=True)
    lo = N_GROUPS + g_idx * EXP_PER_GROUP
    el = jnp.where((lane >= lo) & (lane < lo + EXP_PER_GROUP), logits, neg)
    v1 = jnp.max(el, axis=-1, keepdims=True)
    i1 = jnp.min(jnp.where(el == v1, lane, big), axis=-1, keepdims=True)
    el2 = jnp.where(lane == i1, neg, el)
    v2 = jnp.max(el2, axis=-1, keepdims=True)
    i2 = jnp.min(jnp.where(el2 == v2, lane, big), axis=-1, keepdims=True)
    e2 = jnp.exp(v2 - v1)
    w1 = g_w / (1.0 + e2)
    w2 = g_w * e2 / (1.0 + e2)
    return jnp.where(lane == i1, w1, 0.0) + jnp.where(lane == i2, w2, 0.0), g_idx


MOE_TM = 1024
MOE_RB = 128


def _moe_route_kernel(x_ref, g_ref, wrh_ref, wrl_ref, br_ref, tri_ref, hs_ref, cws_ref, pos_ref, seg_ref):
    tm = x_ref.shape[0]
    h = _rms(x_ref[...], g_ref[...])
    h_hi, h_lo = _split_bf16(h)
    logits = _dot3(h_hi, h_lo, wrh_ref[...], wrl_ref[...]) + br_ref[...]
    cw, g_idx = _moe_route(logits)
    lane = lax.broadcasted_iota(jnp.int32, (tm, LANE), 1).astype(F32)
    onehot = jnp.where(lane == g_idx, 1.0, 0.0)
    before = _dot(tri_ref[...], onehot.astype(BF16))
    counts = jnp.sum(onehot, axis=0, keepdims=True)
    starts = jnp.zeros_like(counts)
    for k in range(1, N_GROUPS + 1):
        below = jnp.sum(jnp.where(lane[0:1] < k, counts, 0.0), axis=-1, keepdims=True)
        starts = starts + jnp.where(lane[0:1] == k, below, 0.0)
    pos = jnp.sum(onehot * (before + starts), axis=-1, keepdims=True)
    pos_b = jnp.broadcast_to(pos, (tm, LANE))
    pos_ref[...] = pos_b
    seg_ref[0] = jnp.broadcast_to(starts, (8, LANE)).astype(jnp.int32)
    pos_row = pos_b.T[0:1]
    row = lax.broadcasted_iota(jnp.int32, (tm, tm), 0).astype(F32)
    perm = jnp.where(row == pos_row, 1.0, 0.0).astype(BF16)
    hs_ref[...] = _dot(perm, h_hi).astype(BF16)
    cw_hi, cw_lo = _split_bf16(cw)
    cws_ref[...] = _dot(perm, cw_hi) + _dot(perm, cw_lo)


def _moe_expert_kernel(seg_ref, hs_ref, cws_ref, wg_ref, wu_ref, wd_ref, o_ref, acc_ref):
    tile, grp = pl.program_id(0), pl.program_id(1)
    tm = hs_ref.shape[0]

    @pl.when(grp == 0)
    def _():
        acc_ref[...] = jnp.zeros_like(acc_ref)

    seg_lo = seg_ref[tile * 8 + grp]
    seg_hi = seg_ref[tile * 8 + grp + 1]
    for r in range(tm // MOE_RB):
        rows = pl.ds(r * MOE_RB, MOE_RB)

        @pl.when((seg_lo < (r + 1) * MOE_RB) & (seg_hi > r * MOE_RB))
        def _():
            h = hs_ref[rows, :]
            cw = cws_ref[rows, :]
            lane = lax.broadcasted_iota(jnp.int32, cw.shape, 1)
            parts = []
            for e in range(EXP_PER_GROUP):
                hid = jax.nn.silu(_dot(h, wg_ref[0, e])) * _dot(h, wu_ref[0, e])
                col = jnp.sum(jnp.where(lane == N_GROUPS + grp * EXP_PER_GROUP + e, cw, 0.0), axis=-1, keepdims=True)
                parts.append((hid * col).astype(BF16))
            acc_ref[rows, :] += _dot(jnp.concatenate(parts, axis=-1), wd_ref[0])

    @pl.when(grp == N_GROUPS - 1)
    def _():
        o_ref[...] = acc_ref[...].astype(BF16)


def _moe_unsort_kernel(x_ref, ys_ref, pos_ref, fn_ref, o_ref, *, final_norm):
    tm = x_ref.shape[0]
    col = lax.broadcasted_iota(jnp.int32, (tm, tm), 1).astype(F32)
    unperm = jnp.where(col == pos_ref[:, 0:1], 1.0, 0.0).astype(BF16)
    y = x_ref[...] + _dot(unperm, ys_ref[...])
    if final_norm:
        y = _rms(y, fn_ref[...])
    o_ref[...] = y


def _moe(x2d, moe_norm, w_group, b_group, w_expert, b_expert, w_gate, w_up, w_down, fnorm, *, final_norm):
    t, d = x2d.shape
    tm = min(MOE_TM, t)
    nt = t // tm
    npad = LANE - N_GROUPS - N_EXPERTS
    wr = jnp.concatenate([w_group, w_expert, jnp.zeros((d, npad), F32)], axis=1)
    br = jnp.concatenate([b_group, b_expert, jnp.zeros((npad,), F32)])[None]
    wrh, wrl = _split_bf16(wr)
    ge = EXP_PER_GROUP * D_FF_E
    wg = w_gate.astype(BF16).reshape(N_GROUPS, EXP_PER_GROUP, d, D_FF_E)
    wu = w_up.astype(BF16).reshape(N_GROUPS, EXP_PER_GROUP, d, D_FF_E)
    wd = w_down.astype(BF16).reshape(N_GROUPS, ge, d)
    tri =(np.arange(tm)[:, None] > np.arange(tm)[None, :]).astype(np.float32)
    full1 = lambda s: pl.BlockSpec(s, lambda i: tuple(0 for _ in s))
    row1 = lambda w: pl.BlockSpec((tm, w), lambda i: (i, 0))
    hs, cws, pos, seg = pl.pallas_call(
        _moe_route_kernel,
        grid=(nt,),
        in_specs=[row1(d), full1((1, d)), full1((d, LANE)), full1((d, LANE)), full1((1, LANE)), full1((tm, tm))],
        out_specs=[row1(d), row1(LANE), row1(LANE), pl.BlockSpec((1, 8, LANE), lambda i: (i, 0, 0))],
        out_shape=[jax.ShapeDtypeStruct((t, d), BF16), jax.ShapeDtypeStruct((t, LANE), F32),
                   jax.ShapeDtypeStruct((t, LANE), F32), jax.ShapeDtypeStruct((nt, 8, LANE), jnp.int32)],
        compiler_params=_cparams(("parallel",)),
        name="moe_route",
    )(x2d, moe_norm, wrh, wrl, br, jnp.asarray(tri, BF16))
    seg_flat = seg[:, 0, :8].reshape(nt * 8)
    ys = pl.pallas_call(
        _moe_expert_kernel,
        grid_spec=pltpu.PrefetchScalarGridSpec(
            num_scalar_prefetch=1,
            grid=(nt, N_GROUPS),
            in_specs=[pl.BlockSpec((tm, d), lambda i, j, s: (i, 0)),
                      pl.BlockSpec((tm, LANE), lambda i, j, s: (i, 0)),
                      pl.BlockSpec((1, EXP_PER_GROUP, d, D_FF_E), lambda i, j, s: (j, 0, 0, 0)),
                      pl.BlockSpec((1, EXP_PER_GROUP, d, D_FF_E), lambda i, j, s: (j, 0, 0, 0)),
                      pl.BlockSpec((1, ge, d), lambda i, j, s: (j, 0, 0))],
            out_specs=pl.BlockSpec((tm, d), lambda i, j, s: (i, 0)),
            scratch_shapes=[pltpu.VMEM((tm, d), F32)]),
        out_shape=jax.ShapeDtypeStruct((t, d), BF16),
        compiler_params=_cparams(("parallel", "arbitrary")),
        name="moe_experts",
    )(seg_flat, hs, cws, wg, wu, wd)
    return pl.pallas_call(
        functools.partial(_moe_unsort_kernel, final_norm=final_norm),
        grid=(nt,),
        in_specs=[row1(d), row1(d), row1(LANE), full1((1, d))],
        out_specs=row1(d),
        out_shape=jax.ShapeDtypeStruct((t, d), F32),
        compiler_params=_cparams(("parallel",)),
        name="moe_unsort",
    )(x2d, ys, pos, fnorm)


def _inproj_weight(w_in):
    d = w_in.shape[0]
    kr = w_in[:, _IN_M + Q_LORA + KV_LORA:_IN_H]
    half = QK_ROPE // 2
    kr_rot = jnp.concatenate([-kr[:, half:], kr[:, :half]], axis=1)
    z64 = jnp.zeros((d, QK_NOPE), F32)
    z32 = jnp.zeros((d, LANE - QK_NOPE - QK_ROPE), F32)
    cols = [w_in[:, _IN_G:], w_in[:, _IN_A:_IN_S], w_in[:, _IN_H:_IN_G], w_in[:, _IN_S:_IN_M],
            w_in[:, _IN_M:_IN_M + Q_LORA], w_in[:, _IN_M + Q_LORA:_IN_M + Q_LORA + KV_LORA],
            z64, kr, z32, z64, kr_rot, z32]
    return jnp.concatenate(cols, axis=1).astype(BF16)


def _rope_tables(positions):
    inv_freq = 1.0 / (ROPE_BASE ** (jnp.arange(0, QK_ROPE, 2, dtype=F32) / QK_ROPE))
    ang = positions.astype(F32)[..., None] * inv_freq
    cos, sin = jnp.cos(ang), jnp.sin(ang)
    shp = positions.shape
    pad = jnp.zeros(shp + (HEAD_PAD - QK_NOPE - QK_ROPE,), F32)
    cos_t = jnp.concatenate([jnp.ones(shp + (QK_NOPE,), F32), cos, cos, pad], axis=-1)
    sin_t = jnp.concatenate([jnp.zeros(shp + (QK_NOPE,), F32), sin, sin, pad], axis=-1)
    return cos_t, sin_t


def kernel(x, mem, positions, mix_norm, w_in, gate_bias, conv_a, w_out_a, s5_lambda_re, s5_lambda_im, s5_log_step, s5_b_re, s5_b_im, s5_c_re, s5_c_im, s5_d, s5_w_glu, mla_q_norm, mla_w_q_b, mla_kv_norm, mla_w_kv_b, mla_w_o, hy_conv, hy_f_w1, hy_f_b1, hy_f_w2, hy_f_b2, hy_f_w3, hy_f_freq, hy_bias, hy_w_out, w_mix_out, xa_norm, mem_norm, xa_w_q, xa_w_kv, xa_w_o, moe_norm, moe_w_group, moe_b_group, moe_w_expert, moe_b_expert, moe_w_gate, moe_w_up, moe_w_down, final_norm):
    b, l, d = x.shape
    depth = w_in.shape[0]
    cos_t, sin_t = _rope_tables(positions)
    tables = _hy_tables(l)
    for i in range(depth):
        p3 = _inproj(x.reshape(b * l, d), mix_norm[i][None], _inproj_weight(w_in[i])).reshape(b, l, N_P)
        a_pre = _shortconv(p3, conv_a[i])
        ys = _s5(p3, _s5_operators(s5_lambda_re[i], s5_lambda_im[i], s5_log_step[i], s5_b_re[i], s5_b_im[i],
                                   s5_c_re[i], s5_c_im[i], s5_d[i]))
        wq, wkv = _mla_weights(mla_w_q_b[i], mla_w_kv_b[i])
        q, k, v = _mla_prep(p3, cos_t, sin_t, mla_q_norm[i][None], mla_kv_norm[i][None], wq, wkv)
        o_mla = _mla_attn(q, k, v)
        kf_re, kf_im = _hy_filter_spectrum(l, hy_f_w1[i], hy_f_b1[i], hy_f_w2[i], hy_f_b2[i], hy_f_w3[i],
                                           hy_f_freq[i], tables)
        z_hy = _hy_conv(p3, hy_conv[i], kf_re, kf_im, hy_bias[i], tables)
        x = _merge(x, p3, a_pre, ys, o_mla, z_hy, gate_bias[i], w_out_a[i], s5_w_glu[i], mla_w_o[i],
                   hy_w_out[i], w_mix_out[i])
        kv = _mem_kv(mem, mem_norm[None], xa_w_kv[i])
        x = _xattn(x, kv, xa_norm[i][None], xa_w_q[i], xa_w_o[i])
        x = _moe(x.reshape(b * l, d), moe_norm[i][None], moe_w_group[i], moe_b_group[i], moe_w_expert[i],
                 moe_b_expert[i], moe_w_gate[i], moe_w_up[i], moe_w_down[i], final_norm[None],
                 final_norm=(i == depth - 1)).reshape(b, l, d)
    return x
```

```python
import functools
import math

import numpy as np
import jax
import jax.numpy as jnp
from jax import lax
from jax.experimental import pallas as pl
from jax.experimental.pallas import tpu as pltpu

F32 = jnp.float32
BF16 = jnp.bfloat16
EPS = 1e-6

D_MODEL = 1024
N_BRANCH = 4
W_A = 512
W_S = 512
S5_GROUP = 16
S5_GROUPS = W_S // S5_GROUP
S5_STATE = 64
S5_CHUNK = 16
MLA_HEADS = 8
Q_LORA = 256
KV_LORA = 256
QK_NOPE = 64
QK_ROPE = 32
V_DIM = 64
ROPE_BASE = 10000.0
W_H = 512
HY_ORDER = 2
HY_EMB = 33
HY_BANDS = (HY_EMB - 1) // 2
HY_FO = 64
HY_FAST_DECAY = 0.3
HY_SLOW_DECAY = 1.5
HY_TARGET = 1e-2
XA_HEADS = 4
XA_DH = 128
N_GROUPS = 4
EXP_PER_GROUP = 8
N_EXPERTS = N_GROUPS * EXP_PER_GROUP
D_FF_E = 256

LANE = 128
HEAD_PAD = 128
VMEM_LIMIT = 56 * 1024 * 1024

OFF_G = 0
OFF_A = OFF_G + N_BRANCH * D_MODEL
OFF_H = OFF_A + 3 * W_A
OFF_S = OFF_H + 3 * W_H
OFF_CQ = OFF_S + W_S
OFF_CKV = OFF_CQ + Q_LORA
OFF_KRA = OFF_CKV + KV_LORA
OFF_KRB = OFF_KRA + LANE
N_P = OFF_KRB + LANE

_IN_A = 0
_IN_S = 3 * W_A
_IN_M = _IN_S + W_S
_IN_H = _IN_M + Q_LORA + KV_LORA + QK_ROPE
_IN_G = _IN_H + 3 * W_H

HY_N1 = 8
HY_CT = 128
HY_NCT = W_H // HY_CT


def _cparams(sem, vmem=VMEM_LIMIT):
    return pltpu.CompilerParams(dimension_semantics=sem, vmem_limit_bytes=vmem)


def _split_bf16(x):
    hi = x.astype(BF16)
    lo = (x - hi.astype(F32)).astype(BF16)
    return hi, lo


def _dot(a, b):
    return jnp.dot(a, b, preferred_element_type=F32)


def _dot3(a_hi, a_lo, b_hi, b_lo):
    return _dot(a_hi, b_hi) + (_dot(a_lo, b_hi) + _dot(a_hi, b_lo))


def _rms(x, g):
    return x * lax.rsqrt(jnp.mean(x * x, axis=-1, keepdims=True) + EPS) * g


def _inproj_kernel(x_ref, g_ref, w_ref, o_ref, h_ref):
    @pl.when(pl.program_id(1) == 0)
    def _():
        h_ref[...] = _rms(x_ref[...], g_ref[...]).astype(BF16)

    o_ref[...] = _dot(h_ref[...], w_ref[...]).astype(o_ref.dtype)


def _inproj(x2d, g, w, *, tm=1024, tn=N_P // 3):
    t = x2d.shape[0]
    return pl.pallas_call(
        _inproj_kernel,
        grid=(t // tm, N_P // tn),
        in_specs=[pl.BlockSpec((tm, D_MODEL), lambda i, j: (i, 0)),
                  pl.BlockSpec((1, D_MODEL), lambda i, j: (0, 0)),
                  pl.BlockSpec((D_MODEL, tn), lambda i, j: (0, j))],
        out_specs=pl.BlockSpec((tm, tn), lambda i, j: (i, j)),
        out_shape=jax.ShapeDtypeStruct((t, N_P), BF16),
        scratch_shapes=[pltpu.VMEM((tm, D_MODEL), BF16)],
        compiler_params=_cparams(("parallel", "arbitrary")),
        name="inproj",
    )(x2d, g, w)


def _conv3(u, w):
    n = u.shape[0]
    row = lax.broadcasted_iota(jnp.int32, u.shape, 0)
    prev = jnp.where(row == 0, 0.0, pltpu.roll(u, 1, axis=0))
    nxt = jnp.where(row == n - 1, 0.0, pltpu.roll(u, n - 1, axis=0))
    return w[0:1] * prev + w[1:2] * u + w[2:3] * nxt


def _shortconv_kernel(bg_ref, cg_ref, xi_ref, wa_ref, a_ref):
    f32 = lambda r: r[0].astype(F32)
    a_ref[0] = (f32(bg_ref) * _conv3(f32(cg_ref) * f32(xi_ref), wa_ref[0])).astype(a_ref.dtype)


def _shortconv(p3, conv_a):
    b, l, _ = p3.shape
    nct = W_A // LANE
    wa = conv_a.reshape(3, nct, LANE).transpose(1, 0, 2)

    def pspec(off):
        return pl.BlockSpec((1, l, LANE), lambda i, j, off=off: (i, 0, off // LANE + j))

    return pl.pallas_call(
        _shortconv_kernel,
        grid=(b, nct),
        in_specs=[pspec(OFF_A), pspec(OFF_A + W_A), pspec(OFF_A + 2 * W_A),
                  pl.BlockSpec((1, 3, LANE), lambda i, j: (j, 0, 0))],
        out_specs=pl.BlockSpec((1, l, LANE), lambda i, j: (i, 0, j)),
        out_shape=jax.ShapeDtypeStruct((b, l, W_A), BF16),
        compiler_params=_cparams(("parallel", "parallel")),
        name="shortconv",
    )(p3, p3, p3, wa)


S5_GPB = LANE // S5_GROUP
S5_NLB = W_S // LANE
S5_XW = S5_CHUNK * LANE
S5_SW = S5_GPB * 2 * S5_STATE


def _s5_operators(lam_re, lam_im, log_step, b_re, b_im, c_re, c_im, d_skip):
    q, hh, g = S5_CHUNK, S5_GROUP, S5_GROUPS
    hp = lax.Precision.HIGHEST
    delta = jnp.exp(log_step)[..., None]

    def powers(exps):
        e = jnp.asarray(exps, F32)
        mag = jnp.exp((lam_re * delta)[..., None] * e)
        ang = (lam_im * delta)[..., None] * e
        return mag * jnp.cos(ang), mag * jnp.sin(ang)

    ramp = np.arange(q)
    p1r, p1i = powers([1.0])
    den = lam_re * lam_re + lam_im * lam_im
    nr, ni = p1r[..., 0] - 1.0, p1i[..., 0]
    fr = (nr * lam_re + ni * lam_im) / den
    fi = (ni * lam_re - nr * lam_im) / den
    bbr = fr[..., None] * b_re - fi[..., None] * b_im
    bbi = fr[..., None] * b_im + fi[..., None] * b_re

    def times_bbar(exps):
        p_r, p_i = powers(exps)
        return (p_r[..., None] * bbr[..., None, :] - p_i[..., None] * bbi[..., None, :],
                p_r[..., None] * bbi[..., None, :] + p_i[..., None] * bbr[..., None, :])

    mr, mi = times_bbar(ramp)
    kern = (jnp.einsum('dghp,dgpek->dgehk', c_re, mr, precision=hp)
            - jnp.einsum('dghp,dgpek->dgehk', c_im, mi, precision=hp))
    i_idx = jnp.arange(q)[:, None]
    j_idx = jnp.arange(q)[None, :]
    diff_f = jnp.clip(j_idx - i_idx, 0, q - 1)
    diff_b = jnp.clip(i_idx - j_idx, 0, q - 1)
    t_f = jnp.where((i_idx <= j_idx)[None, :, :, None, None], kern[0][:, diff_f], 0.0)
    t_b = jnp.where((i_idx >= j_idx)[None, :, :, None, None], kern[1][:, diff_b], 0.0)
    t_all = (t_f + t_b).transpose(0, 1, 4, 2, 3)
    skip = jnp.eye(q, dtype=F32)[:, None, :, None] * jnp.eye(hh, dtype=F32)[None, :, None, :]
    t_all = t_all + skip[None] * d_skip.reshape(g, 1, 1, 1, hh)
    eye_g = jnp.eye(S5_GPB, dtype=BF16)
    blk = lambda a: a.astype(BF16).reshape((S5_NLB, S5_GPB) + a.shape[1:])
    r_idx = lax.broadcasted_iota(jnp.int32, (S5_GPB, q * hh, S5_XW), 1)
    n_idx = lax.broadcasted_iota(jnp.int32, (S5_GPB, q * hh, S5_XW), 2)
    g_idx = lax.broadcasted_iota(jnp.int32, (S5_GPB, q * hh, S5_XW), 0)
    spread = (n_idx == (r_idx // hh) * LANE + g_idx * hh + r_idx % hh).astype(BF16)
    cols_big = lambda a: jnp.einsum('lgrc,gcn->lgrn', blk(a), spread, preferred_element_type=BF16)
    t_big = cols_big(t_all.reshape(g, q * hh, q * hh)).reshape(S5_NLB, S5_GPB, q, hh, S5_XW)
    t_big = t_big.transpose(0, 2, 1, 3, 4).reshape(S5_NLB, S5_XW, S5_XW)

    er, ei = times_bbar(q - 1 - ramp)
    state_in = lambda f, b: jnp.concatenate([f[0].transpose(0, 2, 3, 1), b[1].transpose(0, 2, 3, 1)], axis=-1)

    def st_big(a):
        a5 = blk(a).transpose(0, 2, 1, 3, 4)
        return (a5[:, :, :, :, None, :] * eye_g[None, None, :, None, :, None]).reshape(S5_NLB, S5_XW, S5_SW)

    w1 = jnp.concatenate([t_big, st_big(state_in(er, mr)), st_big(state_in(ei, mi))], axis=-1)

    def coef(c_r, c_i, p_r, p_i):
        cr, ci = c_r.transpose(0, 2, 1)[:, :, None, :], c_i.transpose(0, 2, 1)[:, :, None, :]
        return cr * p_r[..., None] - ci * p_i[..., None], -(cr * p_i[..., None] + ci * p_r[..., None])

    pf_r, pf_i = powers(ramp + 1)
    pb_r, pb_i = powers(q - ramp)
    f_re, f_im = coef(c_re[0], c_im[0], pf_r[0], pf_i[0])
    r_re, r_im = coef(c_re[1], c_im[1], pb_r[1], pb_i[1])
    out_big = lambda a: cols_big(a.reshape(g, 2 * S5_STATE, q * hh)).reshape(S5_NLB, S5_SW, S5_XW)
    wout = jnp.concatenate([out_big(jnp.concatenate([f_re, r_re], axis=1)),
                            out_big(jnp.concatenate([f_im, r_im], axis=1))], axis=1)

    lanes = lambda a: jnp.concatenate([a[0], a[1]], axis=-1).reshape(S5_NLB, S5_SW)
    pq_r, pq_i = powers([float(q)])
    lam = jnp.stack([lanes(pq_r[..., 0]), lanes(pq_i[..., 0])], axis=1)
    return w1, wout, lam


def _s5_kernel(u_ref, w1_ref, wo_ref, lam_ref, y_ref, u32_ref, yin_ref, sre_ref, sim_ref, xre_ref, xim_ref,
               *, n_chunks):
    q, gpb, half = S5_CHUNK, S5_GPB, S5_STATE
    u32_ref[...] = u_ref[0].astype(F32)
    xcat = jnp.concatenate([u32_ref[pl.ds(i, n_chunks, stride=q), :].astype(BF16) for i in range(q)], axis=-1)
    m1 = _dot(xcat, w1_ref[0])
    yin_ref[...] = m1[:, :S5_XW]
    for g in range(gpb):
        sre_ref[pl.ds(g, n_chunks, stride=gpb), :] = m1[:, S5_XW + g * LANE:S5_XW + (g + 1) * LANE]
        sim_ref[pl.ds(g, n_chunks, stride=gpb), :] = m1[:, S5_XW + S5_SW + g * LANE:S5_XW + S5_SW + (g + 1) * LANE]
    lr = lam_ref[0, 0]
    li = lam_ref[0, 1]
    fwd_lane = lax.broadcasted_iota(jnp.int32, (gpb, LANE), 1) < half

    def step(k, carry):
        xr, xi = carry
        rf = pl.multiple_of(k * gpb, gpb)
        rb = pl.multiple_of((n_chunks - 1 - k) * gpb, gpb)
        xre_ref[pl.ds(rf, gpb), 0:half] = xr[:, 0:half]
        xim_ref[pl.ds(rf, gpb), 0:half] = xi[:, 0:half]
        xre_ref[pl.ds(rb, gpb), half:LANE] = xr[:, half:LANE]
        xim_ref[pl.ds(rb, gpb), half:LANE] = xi[:, half:LANE]
        ar = jnp.where(fwd_lane, sre_ref[pl.ds(rf, gpb), :], sre_ref[pl.ds(rb, gpb), :])
        ai = jnp.where(fwd_lane, sim_ref[pl.ds(rf, gpb), :], sim_ref[pl.ds(rb, gpb), :])
        return lr * xr - li * xi + ar, lr * xi + li * xr + ai

    zero = jnp.zeros((gpb, LANE), F32)
    lax.fori_loop(0, n_chunks, step, (zero, zero))
    xs = jnp.concatenate([r[pl.ds(g, n_chunks, stride=gpb), :].astype(BF16)
                          for r in (xre_ref, xim_ref) for g in range(gpb)], axis=-1)
    y = yin_ref[...] + _dot(xs, wo_ref[0])
    for j in range(q):
        y_ref[0, pl.ds(j, n_chunks, stride=q), :] = y[:, j * LANE:(j + 1) * LANE]


def _s5(p3, ops):
    b, l, _ = p3.shape
    nc = l // S5_CHUNK
    w1, wout, lam = ops
    lam = lam.reshape(S5_NLB, 2, S5_GPB, LANE)
    once = dict(pipeline_mode=pl.Buffered(1))
    return pl.pallas_call(
        functools.partial(_s5_kernel, n_chunks=nc),
        grid=(S5_NLB, b),
        in_specs=[pl.BlockSpec((1, l, LANE), lambda j, i: (i, 0, OFF_S // LANE + j)),
                  pl.BlockSpec((1, S5_XW, S5_XW + 2 * S5_SW), lambda j, i: (j, 0, 0), **once),
                  pl.BlockSpec((1, 2 * S5_SW, S5_XW), lambda j, i: (j, 0, 0), **once),
                  pl.BlockSpec((1, 2, S5_GPB, LANE), lambda j, i: (j, 0, 0, 0))],
        out_specs=pl.BlockSpec((1, l, LANE), lambda j, i: (i, 0, j)),
        out_shape=jax.ShapeDtypeStruct((b, l, W_S), F32),
        scratch_shapes=[pltpu.VMEM((l, LANE), F32), pltpu.VMEM((nc, S5_XW), F32)]
        + [pltpu.VMEM((nc * S5_GPB, LANE), F32)] * 4,
        compiler_params=_cparams(("parallel", "arbitrary")),
        name="s5_scan",
    )(p3, w1, wout, lam)


def _mla_prep_kernel(cq_ref, ckv_ref, kra_ref, krb_ref, cos_ref, sin_ref, qn_ref, kvn_ref,
                     wq_ref, wkv_ref, q_ref, k_ref, v_ref):
    cqn = _rms(cq_ref[0].astype(F32), qn_ref[...]).astype(BF16)
    ckvn = _rms(ckv_ref[0].astype(F32), kvn_ref[...]).astype(BF16)
    cos = cos_ref[0]
    sin = sin_ref[0]
    kr = kra_ref[0].astype(F32) * cos + krb_ref[0].astype(F32) * sin
    lane = lax.broadcasted_iota(jnp.int32, cos.shape, 1)
    ones_col = jnp.where(lane == V_DIM, 1.0, 0.0)
    scale = (QK_NOPE + QK_ROPE) ** -0.5
    for h in range(MLA_HEADS):
        qq = _dot(cqn, wq_ref[h])
        q_ref[0, h] = ((qq[:, :HEAD_PAD] * cos + qq[:, HEAD_PAD:] * sin) * scale).astype(BF16)
        kv = _dot(ckvn, wkv_ref[h])
        k_ref[0, h] = (kv[:, :HEAD_PAD] + kr).astype(BF16)
        v_ref[0, h] = (kv[:, HEAD_PAD:] + ones_col).astype(BF16)


def _mla_weights(w_q_b, w_kv_b):
    dq = QK_NOPE + QK_ROPE
    half = QK_ROPE // 2
    wq = w_q_b.reshape(Q_LORA, MLA_HEADS, dq).transpose(1, 0, 2)
    rope = wq[..., QK_NOPE:]
    rot = jnp.concatenate([-rope[..., half:], rope[..., :half]], axis=-1)
    zq = jnp.zeros((MLA_HEADS, Q_LORA, HEAD_PAD - dq), F32)
    zn = jnp.zeros((MLA_HEADS, Q_LORA, QK_NOPE), F32)
    wq_full = jnp.concatenate([wq, zq, zn, rot, zq], axis=-1)
    wkv = w_kv_b.reshape(KV_LORA, MLA_HEADS, QK_NOPE + V_DIM).transpose(1, 0, 2)
    zk = jnp.zeros((MLA_HEADS, KV_LORA, HEAD_PAD - QK_NOPE), F32)
    zv = jnp.zeros((MLA_HEADS, KV_LORA, HEAD_PAD - V_DIM), F32)
    wkv_full = jnp.concatenate([wkv[..., :QK_NOPE], zk, wkv[..., QK_NOPE:], zv], axis=-1)
    return wq_full.astype(BF16), wkv_full.astype(BF16)


def _mla_prep(p3, cos_t, sin_t, q_norm, kv_norm, wq, wkv, *, tl=512):
    b, l, _ = p3.shape
    hspec = pl.BlockSpec((1, MLA_HEADS, tl, HEAD_PAD), lambda i, j: (i, 0, j, 0))
    hshape = jax.ShapeDtypeStruct((b, MLA_HEADS, l, HEAD_PAD), BF16)
    tab = pl.BlockSpec((1, tl, LANE), lambda i, j: (i, j, 0))
    return pl.pallas_call(
        _mla_prep_kernel,
        grid=(b, l // tl),
        in_specs=[pl.BlockSpec((1, tl, Q_LORA), lambda i, j: (i, j, OFF_CQ // Q_LORA)),
                  pl.BlockSpec((1, tl, KV_LORA), lambda i, j: (i, j, OFF_CKV // KV_LORA)),
                  pl.BlockSpec((1, tl, LANE), lambda i, j: (i, j, OFF_KRA // LANE)),
                  pl.BlockSpec((1, tl, LANE), lambda i, j: (i, j, OFF_KRB // LANE)),
                  tab, tab,
                  pl.BlockSpec((1, Q_LORA), lambda i, j: (0, 0)),
                  pl.BlockSpec((1, KV_LORA), lambda i, j: (0, 0)),
                  pl.BlockSpec((MLA_HEADS, Q_LORA, 2 * HEAD_PAD), lambda i, j: (0, 0, 0)),
                  pl.BlockSpec((MLA_HEADS, KV_LORA, 2 * HEAD_PAD), lambda i, j: (0, 0, 0))],
        out_specs=[hspec, hspec, hspec],
        out_shape=[hshape, hshape, hshape],
        compiler_params=_cparams(("parallel", "parallel")),
        name="mla_prep",
    )(p3, p3, p3, p3, cos_t, sin_t, q_norm, kv_norm, wq, wkv)


def _mla_attn_kernel(q_ref, k_ref, v_ref, o_ref, *, tq):
    def q_tile(t, carry):
        rows = pl.ds(pl.multiple_of(t * tq, tq), tq)
        outs = []
        for h in range(2):
            s = lax.dot_general(q_ref[0, h, rows, :], k_ref[0, h], (((1,), (1,)), ((), ())),
                                preferred_element_type=F32)
            m = jnp.max(s, axis=-1, keepdims=True)
            p = jnp.exp(s - m).astype(BF16)
            o = _dot(p, v_ref[0, h])
            outs.append(o / o[:, V_DIM:V_DIM + 1])
        lane = lax.broadcasted_iota(jnp.int32, outs[0].shape, 1)
        o_ref[0, rows, :] = jnp.where(lane < V_DIM, outs[0], pltpu.roll(outs[1], V_DIM, axis=1)).astype(o_ref.dtype)
        return carry

    lax.fori_loop(0, q_ref.shape[2] // tq, q_tile, 0, unroll=2)


def _mla_attn(q, k, v, *, tq=256):
    b, _, l, _ = q.shape
    hspec = pl.BlockSpec((1, 2, l, HEAD_PAD), lambda i, j: (i, j, 0, 0))
    return pl.pallas_call(
        functools.partial(_mla_attn_kernel, tq=tq),
        grid=(b, MLA_HEADS // 2),
        in_specs=[hspec, hspec, hspec],
        out_specs=pl.BlockSpec((1, l, 2 * V_DIM), lambda i, j: (i, 0, j)),
        out_shape=jax.ShapeDtypeStruct((b, l, MLA_HEADS * V_DIM), BF16),
        compiler_params=_cparams(("parallel", "parallel")),
        name="mla_attn",
    )(q, k, v)


def _hy_sizes(l):
    n = 2 * l
    n2 = n // HY_N1
    nf = n2 // 2 + 1
    nfp = ((nf + 63) // 64) * 64
    return n, n2, nf, nfp


def _hy_tables(l):
    n, n2, nf, nfp = _hy_sizes(l)
    f2 = np.arange(nfp)[:, None].astype(np.float64)
    t2 = np.arange(n2 // 2)[None, :].astype(np.float64)
    valid = (np.arange(nfp) < nf)[:, None]
    ang = 2.0 * np.pi * f2 * t2 / n2
    fwd = np.concatenate([np.where(valid, np.cos(ang), 0.0), np.where(valid, -np.sin(ang), 0.0)], axis=0)
    wgt = np.where((np.arange(nfp) == 0) | (np.arange(nfp) == nf - 1), 1.0, 2.0)[:, None] * valid / n
    inv = np.concatenate([(wgt * np.cos(ang)).T, (-wgt * np.sin(ang)).T], axis=1)
    t1 = np.arange(HY_N1)[None, :].astype(np.float64)
    tw_ang = 2.0 * np.pi * f2 * t1 / n
    tw_re = np.repeat(np.cos(tw_ang), HY_CT, axis=1).astype(np.float32)
    tw_im = np.repeat(-np.sin(tw_ang), HY_CT, axis=1).astype(np.float32)
    fwd_hi, fwd_lo = _split_bf16(jnp.asarray(fwd, F32))
    return fwd_hi, fwd_lo, jnp.asarray(inv, F32).astype(BF16), jnp.asarray(tw_re), jnp.asarray(tw_im)


def _cmul(a, b):
    return a[0] * b[0] - a[1] * b[1], a[0] * b[1] + a[1] * b[0]


def _cadd(a, b):
    return a[0] + b[0], a[1] + b[1]


def _csub(a, b):
    return a[0] - b[0], a[1] - b[1]


def _cmul_i(a, sign):
    return (-a[1], a[0]) if sign > 0 else (a[1], -a[0])


def _fft4(a, sign):
    s0, s1 = _cadd(a[0], a[2]), _csub(a[0], a[2])
    s2, s3 = _cadd(a[1], a[3]), _csub(a[1], a[3])
    r3 = _cmul_i(s3, sign)
    return [_cadd(s0, s2), _cadd(s1, r3), _csub(s0, s2), _csub(s1, r3)]


def _fft8(x, sign):
    e = _fft4([x[0], x[2], x[4], x[6]], sign)
    o = _fft4([x[1], x[3], x[5], x[7]], sign)
    r = math.sqrt(0.5)
    o1 = ((o[1][0] - sign * o[1][1]) * r, (o[1][1] + sign * o[1][0]) * r)
    o2 = _cmul_i(o[2], sign)
    o3 = ((-o[3][0] - sign * o[3][1]) * r, (-o[3][1] + sign * o[3][0]) * r)
    tw = [o[0], o1, o2, o3]
    return [_cadd(e[k], tw[k]) for k in range(4)] + [_csub(e[k], tw[k]) for k in range(4)]


def _blocks(ref_re, ref_im, rows):
    return [(ref_re[rows, k * HY_CT:(k + 1) * HY_CT], ref_im[rows, k * HY_CT:(k + 1) * HY_CT])
            for k in range(HY_N1)]


def _hy_spectrum(z_ref, twr_ref, twi_ref, rows, nfp):
    t = []
    for k in range(HY_N1):
        sl = slice(k * HY_CT, (k + 1) * HY_CT)
        zk = (z_ref[rows, sl], z_ref[pl.ds(pl.multiple_of(nfp + rows.start, rows.size), rows.size), sl])
        t.append(_cmul(zk, (twr_ref[rows, sl], twi_ref[rows, sl])))
    return _fft8(t, -1)


def _hy_fold(nat_ref, rows):
    return jnp.concatenate([nat_ref[pl.ds(t1, rows, stride=HY_N1), :] for t1 in range(HY_N1)], axis=-1)


def _hy_filter_kernel(z_ref, w1_ref, b1_ref, w2_ref, b2_ref, w3_ref, fr_ref, tn_ref, dl_ref,
                      fh_ref, fl_ref, twr_ref, twi_ref, kr_ref, ki_ref, ff_ref, fb_ref, zf_ref, zb_ref,
                      *, nfp, row_chunk, rows):
    hp = lax.Precision.HIGHEST
    fr = fr_ref[...]
    h = jnp.sin(fr * (jnp.dot(z_ref[...], w1_ref[...], precision=hp, preferred_element_type=F32) + b1_ref[...]))
    h = jnp.sin(fr * (jnp.dot(h, w2_ref[...], precision=hp, preferred_element_type=F32) + b2_ref[...]))
    filt = jnp.dot(h, w3_ref[0, 0], precision=hp, preferred_element_type=F32)
    decay = jnp.exp(-tn_ref[...] * dl_ref[0])
    fwd = filt[:, :HY_CT] * decay
    row = lax.broadcasted_iota(jnp.int32, fwd.shape, 0)
    bwd = jnp.where(row == 0, 0.0, filt[:, HY_CT:] * decay)
    inv = lax.rsqrt(jnp.sum(fwd * fwd, axis=0, keepdims=True) + jnp.sum(bwd * bwd, axis=0, keepdims=True) + EPS)
    ff_ref[...] = fwd
    fb_ref[...] = bwd
    for src, dst in ((ff_ref, zf_ref), (fb_ref, zb_ref)):
        k_hi, k_lo = _split_bf16(_hy_fold(src, rows))
        dst[...] = _dot3(fh_ref[...], fl_ref[...], k_hi, k_lo)
    def chunk(c, carry):
        rws = pl.ds(pl.multiple_of(c * row_chunk, row_chunk), row_chunk)
        sf = _hy_spectrum(zf_ref, twr_ref, twi_ref, rws, nfp)
        sb = _hy_spectrum(zb_ref, twr_ref, twi_ref, rws, nfp)
        for f1 in range(HY_N1):
            sl = slice(f1 * HY_CT, (f1 + 1) * HY_CT)
            kr_ref[0, 0, rws, sl] = (sf[f1][0] + sb[f1][0]) * inv
            ki_ref[0, 0, rws, sl] = (sf[f1][1] - sb[f1][1]) * inv
        return carry

    lax.fori_loop(0, nfp // row_chunk, chunk, 0)


HY_ROW_CHUNK = 16


def _hy_row_chunk(nfp):
    assert nfp % HY_ROW_CHUNK == 0
    return HY_ROW_CHUNK


def _hy_conv_chunk(nfp):
    return next(c for c in (96, 64, 48, 32, 16) if nfp % c == 0)


def _hy_filter_spectrum(l, w1, b1, w2, b2, w3, freq, tables):
    n, n2, nf, nfp = _hy_sizes(l)
    fwd_hi, fwd_lo, _, tw_re, tw_im = tables
    rows = l // HY_N1
    wide = HY_N1 * HY_CT
    t = np.arange(l, dtype=np.float32)
    t_norm = t / np.float32(max(l - 1, 1))
    bands = np.linspace(1e-4, HY_BANDS - 1, HY_BANDS, dtype=np.float32)
    ang = np.float32(2.0 * math.pi / l) * t[:, None] * bands[None]
    z = np.concatenate([t_norm[:, None], np.cos(ang), -np.sin(ang)], axis=-1).astype(np.float32)
    kpad = 40
    z = np.pad(z, ((0, 0), (0, kpad - HY_EMB)))
    w1p = jnp.pad(w1, ((0, kpad - HY_EMB), (0, 0)))
    max_decay = math.log(HY_TARGET) / HY_FAST_DECAY
    min_decay = math.log(HY_TARGET) / HY_SLOW_DECAY
    deltas = np.abs(np.linspace(min_decay, max_decay, W_H, dtype=np.float32)).reshape(HY_NCT, 1, HY_CT)
    w3t = w3.reshape(HY_FO, HY_ORDER, 2, HY_NCT, HY_CT).transpose(1, 3, 0, 2, 4).reshape(HY_ORDER, HY_NCT, HY_FO, 2 * HY_CT)
    full = lambda s: pl.BlockSpec(s, lambda i, j: (0, 0))
    ospec = pl.BlockSpec((1, 1, nfp, wide), lambda i, j: (i, j, 0, 0))
    oshape = jax.ShapeDtypeStruct((HY_ORDER, HY_NCT, nfp, wide), F32)
    return pl.pallas_call(
        functools.partial(_hy_filter_kernel, nfp=nfp, row_chunk=_hy_row_chunk(nfp), rows=rows),
        grid=(HY_ORDER, HY_NCT),
        in_specs=[full((l, kpad)), full((kpad, HY_FO)), full((1, HY_FO)), full((HY_FO, HY_FO)), full((1, HY_FO)),
                  pl.BlockSpec((1, 1, HY_FO, 2 * HY_CT), lambda i, j: (i, j, 0, 0)),
                  full((1, HY_FO)), full((l, 1)),
                  pl.BlockSpec((1, 1, HY_CT), lambda i, j: (j, 0, 0)),
                  full((2 * nfp, rows)), full((2 * nfp, rows)), full((nfp, wide)), full((nfp, wide))],
        out_specs=[ospec, ospec],
        out_shape=[oshape, oshape],
        scratch_shapes=[pltpu.VMEM((l, HY_CT), F32), pltpu.VMEM((l, HY_CT), F32),
                        pltpu.VMEM((2 * nfp, wide), F32), pltpu.VMEM((2 * nfp, wide), F32)],
        compiler_params=_cparams(("parallel", "parallel")),
        name="hyena_filter",
    )(jnp.asarray(z), w1p, b1[None], w2, b2[None], w3t, freq[None], jnp.asarray(t_norm[:, None]),
      jnp.asarray(deltas), fwd_hi, fwd_lo, tw_re, tw_im)


def _hy_conv_kernel(pv_ref, p1_ref, p2_ref, wc_ref, kr_ref, ki_ref, bias_ref, fwd_ref, inv_ref, twr_ref, twi_ref,
                    o_ref, nat_ref, z_ref, u_ref, *, nfp, row_chunk, rows):
    def folded_conv3(p_ref, part):
        nat_ref[...] = _conv3(p_ref[0].astype(F32), wc_ref[0, part])
        return _hy_fold(nat_ref, rows)

    def long_conv(u, order):
        z_ref[...] = _dot(fwd_ref[...], u.astype(BF16))
        for c in range(nfp // row_chunk):
            rws = pl.ds(c * row_chunk, row_chunk)
            spec = _hy_spectrum(z_ref, twr_ref, twi_ref, rws, nfp)
            kf = _blocks(kr_ref.at[order, 0], ki_ref.at[order, 0], rws)
            y = _fft8([_cmul(spec[f1], kf[f1]) for f1 in range(HY_N1)], +1)
            for t1 in range(HY_N1):
                sl = slice(t1 * HY_CT, (t1 + 1) * HY_CT)
                w = _cmul(y[t1], (twr_ref[rws, sl], -twi_ref[rws, sl]))
                u_ref[rws, sl] = w[0].astype(BF16)
                u_ref[pl.ds(nfp + c * row_chunk, row_chunk), sl] = w[1].astype(BF16)
        return _dot(inv_ref[...], u_ref[...])

    v = folded_conv3(pv_ref, 0)
    x1 = folded_conv3(p1_ref, 1)
    x2 = folded_conv3(p2_ref, 2)
    bias = bias_ref[0]
    z1 = x1 * (long_conv(v, 0) + v * bias[0:1])
    z2 = x2 * (long_conv(z1, 1) + z1 * bias[1:2])
    for t1 in range(HY_N1):
        o_ref[0, pl.ds(t1, rows, stride=HY_N1), :] = z2[:, t1 * HY_CT:(t1 + 1) * HY_CT]


def _hy_conv(p3, hy_conv, kf_re, kf_im, bias, tables):
    b, l, _ = p3.shape
    n, n2, nf, nfp = _hy_sizes(l)
    fwd_hi, _, inv_t, tw_re, tw_im = tables
    rows = l // HY_N1
    wide = HY_N1 * HY_CT
    wc = hy_conv.reshape(3, 3, HY_NCT, HY_CT).transpose(2, 1, 0, 3)
    bias_t = jnp.tile(bias.reshape(HY_ORDER, HY_NCT, 1, HY_CT), (1, 1, HY_N1, 1))
    bias_t = bias_t.transpose(1, 0, 2, 3).reshape(HY_NCT, HY_ORDER, wide)
    once = dict(pipeline_mode=pl.Buffered(1))

    def pspec(part):
        return pl.BlockSpec((1, l, HY_CT), lambda j, i, part=part: (i, 0, (OFF_H + part * W_H) // HY_CT + j))

    kspec = pl.BlockSpec((HY_ORDER, 1, nfp, wide), lambda j, i: (0, j, 0, 0), **once)
    full = lambda s: pl.BlockSpec(s, lambda j, i: (0, 0), **once)
    return pl.pallas_call(
        functools.partial(_hy_conv_kernel, nfp=nfp, row_chunk=_hy_conv_chunk(nfp), rows=rows),
        grid=(HY_NCT, b),
        in_specs=[pspec(0), pspec(1), pspec(2),
                  pl.BlockSpec((1, 3, 3, HY_CT), lambda j, i: (j, 0, 0, 0)),
                  kspec, kspec,
                  pl.BlockSpec((1, HY_ORDER, wide), lambda j, i: (j, 0, 0)),
                  full((2 * nfp, rows)), full((rows, 2 * nfp)), full((nfp, wide)), full((nfp, wide))],
        out_specs=pl.BlockSpec((1, l, HY_CT), lambda j, i: (i, 0, j)),
        out_shape=jax.ShapeDtypeStruct((b, l, W_H), F32),
        scratch_shapes=[pltpu.VMEM((l, HY_CT), F32), pltpu.VMEM((2 * nfp, wide), F32),
                        pltpu.VMEM((2 * nfp, wide), BF16)],
        compiler_params=_cparams(("parallel", "arbitrary")),
        name="hyena_conv",
    )(p3, p3, p3, wc, kf_re, kf_im, bias_t, fwd_hi, inv_t, tw_re, tw_im)


def _merge_kernel(x_ref, pg_ref, a_ref, ys_ref, om_ref, zh_ref, gb_ref, wa_ref, wglu_ref, wo_ref,
                  wh_ref, wmix_ref, o_ref):
    d = D_MODEL
    y_a = _dot(a_ref[0].astype(BF16), wa_ref[...])
    glu = _dot(jax.nn.gelu(ys_ref[0]).astype(BF16), wglu_ref[...])
    y_s = glu[:, :d] * jax.nn.sigmoid(glu[:, d:])
    y_m = _dot(om_ref[0].astype(BF16), wo_ref[...])
    y_h = _dot(zh_ref[0].astype(BF16), wh_ref[...])
    gb = gb_ref[...]
    merged = jnp.zeros_like(y_a)
    for i, y in enumerate((y_a, y_s, y_m, y_h)):
        merged = merged + jax.nn.sigmoid(pg_ref[0, :, i * d:(i + 1) * d].astype(F32) + gb[i:i + 1]) * y
    o_ref[0] = x_ref[0] + _dot(merged.astype(BF16), wmix_ref[...])


def _merge(x, p3, a_pre, ys, o_mla, z_hy, gate_bias, w_out_a, w_glu, w_o, hy_w_out, w_mix, *, tm=512):
    b, l, d = x.shape
    row = lambda w: pl.BlockSpec((1, tm, w), lambda i, j: (i, j, 0))
    full = lambda s: pl.BlockSpec(s, lambda i, j: tuple(0 for _ in s))
    return pl.pallas_call(
        _merge_kernel,
        grid=(b, l // tm),
        in_specs=[row(d), row(N_BRANCH * d), row(W_A), row(W_S), row(MLA_HEADS * V_DIM), row(W_H),
                  full((N_BRANCH, d)), full((W_A, d)), full((W_S, 2 * d)), full((MLA_HEADS * V_DIM, d)),
                  full((W_H, d)), full((d, d))],
        out_specs=row(d),
        out_shape=jax.ShapeDtypeStruct((b, l, d), F32),
        compiler_params=_cparams(("parallel", "parallel")),
        name="merge",
    )(x, p3, a_pre, ys, o_mla, z_hy, gate_bias, w_out_a.astype(BF16), w_glu.astype(BF16),
      w_o.astype(BF16), hy_w_out.astype(BF16), w_mix.astype(BF16))


def _mem_kv_kernel(m_ref, g_ref, w_ref, o_ref):
    o_ref[0] = _dot(_rms(m_ref[0], g_ref[...]).astype(BF16), w_ref[...]).astype(BF16)


def _mem_kv(mem, mem_norm, w_kv):
    b, m, d = mem.shape
    n = w_kv.shape[1]
    return pl.pallas_call(
        _mem_kv_kernel,
        grid=(b,),
        in_specs=[pl.BlockSpec((1, m, d), lambda i: (i, 0, 0)),
                  pl.BlockSpec((1, d), lambda i: (0, 0)),
                  pl.BlockSpec((d, n), lambda i: (0, 0))],
        out_specs=pl.BlockSpec((1, m, n), lambda i: (i, 0, 0)),
        out_shape=jax.ShapeDtypeStruct((b, m, n), BF16),
        compiler_params=_cparams(("parallel",)),
        name="mem_kv",
    )(mem, mem_norm, w_kv.astype(BF16))


def _xattn_kernel(x_ref, g_ref, kv_ref, wq_ref, wo_ref, o_ref):
    x = x_ref[0]
    h = _rms(x, g_ref[...]).astype(BF16)
    q = (_dot(h, wq_ref[...]) * (XA_DH ** -0.5)).astype(BF16)
    outs = []
    for hd in range(XA_HEADS):
        k = kv_ref[0, :, hd * 2 * XA_DH:hd * 2 * XA_DH + XA_DH]
        v = kv_ref[0, :, hd * 2 * XA_DH + XA_DH:(hd + 1) * 2 * XA_DH]
        s = lax.dot_general(q[:, hd * XA_DH:(hd + 1) * XA_DH], k, (((1,), (1,)), ((), ())),
                            preferred_element_type=F32)
        e = jnp.exp(s - jnp.max(s, axis=-1, keepdims=True))
        p = (e / jnp.sum(e, axis=-1, keepdims=True)).astype(BF16)
        outs.append(_dot(p, v))
    o = jnp.concatenate(outs, axis=-1).astype(BF16)
    o_ref[0] = x + _dot(o, wo_ref[...])


def _xattn(x, kv, xa_norm, w_q, w_o, *, tm=512):
    b, l, d = x.shape
    m, n = kv.shape[1], kv.shape[2]
    full = lambda s: pl.BlockSpec(s, lambda i, j: tuple(0 for _ in s))
    return pl.pallas_call(
        _xattn_kernel,
        grid=(b, l // tm),
        in_specs=[pl.BlockSpec((1, tm, d), lambda i, j: (i, j, 0)), full((1, d)),
                  pl.BlockSpec((1, m, n), lambda i, j: (i, 0, 0)),
                  full((d, XA_HEADS * XA_DH)), full((XA_HEADS * XA_DH, d))],
        out_specs=pl.BlockSpec((1, tm, d), lambda i, j: (i, j, 0)),
        out_shape=jax.ShapeDtypeStruct((b, l, d), F32),
        compiler_params=_cparams(("parallel", "parallel")),
        name="xattn",
    )(x, xa_norm, kv, w_q.astype(BF16), w_o.astype(BF16))


def _moe_route(logits):
    neg = -jnp.inf
    big = float(1 << 20)
    lane = lax.broadcasted_iota(jnp.int32, logits.shape, 1).astype(F32)
    gl = jnp.where(lane < N_GROUPS, logits, neg)
    gmax = jnp.max(gl, axis=-1, keepdims=True)
    g_idx = jnp.min(jnp.where(gl == gmax, lane, big), axis=-1, keepdims=True)
    g_w = 1.0 / jnp.sum(jnp.exp(gl - gmax), axis=-1, keepdims=True)
    lo = N_GROUPS + g_idx * EXP_PER_GROUP
    el = jnp.where((lane >= lo) & (lane < lo + EXP_PER_GROUP), logits, neg)
    v1 = jnp.max(el, axis=-1, keepdims=True)
    i1 = jnp.min(jnp.where(el == v1, lane, big), axis=-1, keepdims=True)
    el2 = jnp.where(lane == i1, neg, el)
    v2 = jnp.max(el2, axis=-1, keepdims=True)
    i2 = jnp.min(jnp.where(el2 == v2, lane, big), axis=-1, keepdims=True)
    e2 = jnp.exp(v2 - v1)
    w1 = g_w / (1.0 + e2)
    w2 = g_w * e2 / (1.0 + e2)
    return jnp.where(lane == i1, w1, 0.0) + jnp.where(lane == i2, w2, 0.0), g_idx


MOE_TM = 1024
MOE_RB = 128


def _moe_route_kernel(x_ref, g_ref, wrh_ref, wrl_ref, br_ref, tri_ref, hs_ref, cws_ref, pos_ref, seg_ref):
    tm = x_ref.shape[0]
    h = _rms(x_ref[...], g_ref[...])
    h_hi, h_lo = _split_bf16(h)
    logits = _dot3(h_hi, h_lo, wrh_ref[...], wrl_ref[...]) + br_ref[...]
    cw, g_idx = _moe_route(logits)
    lane = lax.broadcasted_iota(jnp.int32, (tm, LANE), 1).astype(F32)
    onehot = jnp.where(lane == g_idx, 1.0, 0.0)
    before = _dot(tri_ref[...], onehot.astype(BF16))
    counts = jnp.sum(onehot, axis=0, keepdims=True)
    starts = jnp.zeros_like(counts)
    for k in range(1, N_GROUPS + 1):
        below = jnp.sum(jnp.where(lane[0:1] < k, counts, 0.0), axis=-1, keepdims=True)
        starts = starts + jnp.where(lane[0:1] == k, below, 0.0)
    pos = jnp.sum(onehot * (before + starts), axis=-1, keepdims=True)
    pos_b = jnp.broadcast_to(pos, (tm, LANE))
    pos_ref[...] = pos_b
    seg_ref[0] = jnp.broadcast_to(starts, (8, LANE)).astype(jnp.int32)
    pos_row = pos_b.T[0:1]
    row = lax.broadcasted_iota(jnp.int32, (tm, tm), 0).astype(F32)
    perm = jnp.where(row == pos_row, 1.0, 0.0).astype(BF16)
    hs_ref[...] = _dot(perm, h_hi).astype(BF16)
    cw_hi, cw_lo = _split_bf16(cw)
    cws_ref[...] = _dot(perm, cw_hi) + _dot(perm, cw_lo)


def _moe_expert_kernel(seg_ref, hs_ref, cws_ref, wg_ref, wu_ref, wd_ref, o_ref, acc_ref):
    tile, grp = pl.program_id(0), pl.program_id(1)
    tm = hs_ref.shape[0]

    @pl.when(grp == 0)
    def _():
        acc_ref[...] = jnp.zeros_like(acc_ref)

    seg_lo = seg_ref[tile * 8 + grp]
    seg_hi = seg_ref[tile * 8 + grp + 1]
    for r in range(tm // MOE_RB):
        rows = pl.ds(r * MOE_RB, MOE_RB)

        @pl.when((seg_lo < (r + 1) * MOE_RB) & (seg_hi > r * MOE_RB))
        def _():
            h = hs_ref[rows, :]
            cw = cws_ref[rows, :]
            lane = lax.broadcasted_iota(jnp.int32, cw.shape, 1)
            parts = []
            for e in range(EXP_PER_GROUP):
                hid = jax.nn.silu(_dot(h, wg_ref[0, e])) * _dot(h, wu_ref[0, e])
                col = jnp.sum(jnp.where(lane == N_GROUPS + grp * EXP_PER_GROUP + e, cw, 0.0), axis=-1, keepdims=True)
                parts.append((hid * col).astype(BF16))
            acc_ref[rows, :] += _dot(jnp.concatenate(parts, axis=-1), wd_ref[0])

    @pl.when(grp == N_GROUPS - 1)
    def _():
        o_ref[...] = acc_ref[...].astype(BF16)


def _moe_unsort_kernel(x_ref, ys_ref, pos_ref, fn_ref, o_ref, *, final_norm):
    tm = x_ref.shape[0]
    col = lax.broadcasted_iota(jnp.int32, (tm, tm), 1).astype(F32)
    unperm = jnp.where(col == pos_ref[:, 0:1], 1.0, 0.0).astype(BF16)
    y = x_ref[...] + _dot(unperm, ys_ref[...])
    if final_norm:
        y = _rms(y, fn_ref[...])
    o_ref[...] = y


def _moe(x2d, moe_norm, w_group, b_group, w_expert, b_expert, w_gate, w_up, w_down, fnorm, *, final_norm):
    t, d = x2d.shape
    tm = min(MOE_TM, t)
    nt = t // tm
    npad = LANE - N_GROUPS - N_EXPERTS
    wr = jnp.concatenate([w_group, w_expert, jnp.zeros((d, npad), F32)], axis=1)
    br = jnp.concatenate([b_group, b_expert, jnp.zeros((npad,), F32)])[None]
    wrh, wrl = _split_bf16(wr)
    ge = EXP_PER_GROUP * D_FF_E
    wg = w_gate.astype(BF16).reshape(N_GROUPS, EXP_PER_GROUP, d, D_FF_E)
    wu = w_up.astype(BF16).reshape(N_GROUPS, EXP_PER_GROUP, d, D_FF_E)
    wd = w_down.astype(BF16).reshape(N_GROUPS, ge, d)
    tri =(np.arange(tm)[:, None] > np.arange(tm)[None, :]).astype(np.float32)
    full1 = lambda s: pl.BlockSpec(s, lambda i: tuple(0 for _ in s))
    row1 = lambda w: pl.BlockSpec((tm, w), lambda i: (i, 0))
    hs, cws, pos, seg = pl.pallas_call(
        _moe_route_kernel,
        grid=(nt,),
        in_specs=[row1(d), full1((1, d)), full1((d, LANE)), full1((d, LANE)), full1((1, LANE)), full1((tm, tm))],
        out_specs=[row1(d), row1(LANE), row1(LANE), pl.BlockSpec((1, 8, LANE), lambda i: (i, 0, 0))],
        out_shape=[jax.ShapeDtypeStruct((t, d), BF16), jax.ShapeDtypeStruct((t, LANE), F32),
                   jax.ShapeDtypeStruct((t, LANE), F32), jax.ShapeDtypeStruct((nt, 8, LANE), jnp.int32)],
        compiler_params=_cparams(("parallel",)),
        name="moe_route",
    )(x2d, moe_norm, wrh, wrl, br, jnp.asarray(tri, BF16))
    seg_flat = seg[:, 0, :8].reshape(nt * 8)
    ys = pl.pallas_call(
        _moe_expert_kernel,
        grid_spec=pltpu.PrefetchScalarGridSpec(
            num_scalar_prefetch=1,
            grid=(nt, N_GROUPS),
            in_specs=[pl.BlockSpec((tm, d), lambda i, j, s: (i, 0)),
                      pl.BlockSpec((tm, LANE), lambda i, j, s: (i, 0)),
                      pl.BlockSpec((1, EXP_PER_GROUP, d, D_FF_E), lambda i, j, s: (j, 0, 0, 0)),
                      pl.BlockSpec((1, EXP_PER_GROUP, d, D_FF_E), lambda i, j, s: (j, 0, 0, 0)),
                      pl.BlockSpec((1, ge, d), lambda i, j, s: (j, 0, 0))],
            out_specs=pl.BlockSpec((tm, d), lambda i, j, s: (i, 0)),
            scratch_shapes=[pltpu.VMEM((tm, d), F32)]),
        out_shape=jax.ShapeDtypeStruct((t, d), BF16),
        compiler_params=_cparams(("parallel", "arbitrary")),
        name="moe_experts",
    )(seg_flat, hs, cws, wg, wu, wd)
    return pl.pallas_call(
        functools.partial(_moe_unsort_kernel, final_norm=final_norm),
        grid=(nt,),
        in_specs=[row1(d), row1(d), row1(LANE), full1((1, d))],
        out_specs=row1(d),
        out_shape=jax.ShapeDtypeStruct((t, d), F32),
        compiler_params=_cparams(("parallel",)),
        name="moe_unsort",
    )(x2d, ys, pos, fnorm)


def _inproj_weight(w_in):
    d = w_in.shape[0]
    kr = w_in[:, _IN_M + Q_LORA + KV_LORA:_IN_H]
    half = QK_ROPE // 2
    kr_rot = jnp.concatenate([-kr[:, half:], kr[:, :half]], axis=1)
    z64 = jnp.zeros((d, QK_NOPE), F32)
    z32 = jnp.zeros((d, LANE - QK_NOPE - QK_ROPE), F32)
    cols = [w_in[:, _IN_G:], w_in[:, _IN_A:_IN_S], w_in[:, _IN_H:_IN_G], w_in[:, _IN_S:_IN_M],
            w_in[:, _IN_M:_IN_M + Q_LORA], w_in[:, _IN_M + Q_LORA:_IN_M + Q_LORA + KV_LORA],
            z64, kr, z32, z64, kr_rot, z32]
    return jnp.concatenate(cols, axis=1).astype(BF16)


def _rope_tables(positions):
    inv_freq = 1.0 / (ROPE_BASE ** (jnp.arange(0, QK_ROPE, 2, dtype=F32) / QK_ROPE))
    ang = positions.astype(F32)[..., None] * inv_freq
    cos, sin = jnp.cos(ang), jnp.sin(ang)
    shp = positions.shape
    pad = jnp.zeros(shp + (HEAD_PAD - QK_NOPE - QK_ROPE,), F32)
    cos_t = jnp.concatenate([jnp.ones(shp + (QK_NOPE,), F32), cos, cos, pad], axis=-1)
    sin_t = jnp.concatenate([jnp.zeros(shp + (QK_NOPE,), F32), sin, sin, pad], axis=-1)
    return cos_t, sin_t


def kernel(x, mem, positions, mix_norm, w_in, gate_bias, conv_a, w_out_a, s5_lambda_re, s5_lambda_im, s5_log_step, s5_b_re, s5_b_im, s5_c_re, s5_c_im, s5_d, s5_w_glu, mla_q_norm, mla_w_q_b, mla_kv_norm, mla_w_kv_b, mla_w_o, hy_conv, hy_f_w1, hy_f_b1, hy_f_w2, hy_f_b2, hy_f_w3, hy_f_freq, hy_bias, hy_w_out, w_mix_out, xa_norm, mem_norm, xa_w_q, xa_w_kv, xa_w_o, moe_norm, moe_w_group, moe_b_group, moe_w_expert, moe_b_expert, moe_w_gate, moe_w_up, moe_w_down, final_norm):
    b, l, d = x.shape
    depth = w_in.shape[0]
    cos_t, sin_t = _rope_tables(positions)
    tables = _hy_tables(l)
    for i in range(depth):
        p3 = _inproj(x.reshape(b * l, d), mix_norm[i][None], _inproj_weight(w_in[i])).reshape(b, l, N_P)
        a_pre = _shortconv(p3, conv_a[i])
        ys = _s5(p3, _s5_operators(s5_lambda_re[i], s5_lambda_im[i], s5_log_step[i], s5_b_re[i], s5_b_im[i],
                                   s5_c_re[i], s5_c_im[i], s5_d[i]))
        wq, wkv = _mla_weights(mla_w_q_b[i], mla_w_kv_b[i])
        q, k, v = _mla_prep(p3, cos_t, sin_t, mla_q_norm[i][None], mla_kv_norm[i][None], wq, wkv)
        o_mla = _mla_attn(q, k, v)
        kf_re, kf_im = _hy_filter_spectrum(l, hy_f_w1[i], hy_f_b1[i], hy_f_w2[i], hy_f_b2[i], hy_f_w3[i],
                                           hy_f_freq[i], tables)
        z_hy = _hy_conv(p3, hy_conv[i], kf_re, kf_im, hy_bias[i], tables)
        x = _merge(x, p3, a_pre, ys, o_mla, z_hy, gate_bias[i], w_out_a[i], s5_w_glu[i], mla_w_o[i],
                   hy_w_out[i], w_mix_out[i])
        kv = _mem_kv(mem, mem_norm[None], xa_w_kv[i])
        x = _xattn(x, kv, xa_norm[i][None], xa_w_q[i], xa_w_o[i])
        x = _moe(x.reshape(b * l, d), moe_norm[i][None], moe_w_group[i], moe_b_group[i], moe_w_expert[i],
                 moe_b_expert[i], moe_w_gate[i], moe_w_up[i], moe_w_down[i], final_norm[None],
                 final_norm=(i == depth - 1)).reshape(b, l, d)
    return x
```
